```python
import jax, jax.numpy as jnp
from jax import lax
import numpy as np

D_MODEL = 2048
BATCH = 16
SEQ = 2048
DEPTH = 1

PLE_DIM = 256
POOL_WIDTH = D_MODEL // 2
POOL_WINDOWS = (2, 4, 8, 16)
POOL_GROUP = POOL_WIDTH // len(POOL_WINDOWS)
HEAD_DIM = 64
N_HEADS = (D_MODEL // 2) // HEAD_DIM
ATTN_WIDTH = N_HEADS * HEAD_DIM
MIX_WIDTH = POOL_WIDTH + ATTN_WIDTH
Q_BLOCK = 128
N_EXPERTS = 32
TOP_K = 4
D_EXPERT = D_MODEL
SWIGLU_LIMIT = 7.0
SWIGLU_ALPHA = 1.702
EXPERT_BLOCK = 256
EPS = 1e-6

kernel_name = 'hybrid_pool_stickbreak_moe_ple'


def rmsnorm(x, g):
    xf = x.astype(jnp.float32)
    y = xf * lax.rsqrt(jnp.mean(xf * xf, axis=-1, keepdims=True) + EPS)
    return (y * g.astype(jnp.float32)).astype(x.dtype)


def pool_mixer(u, pool_w, pool_scale):
    s = u.shape[1]
    uf = u.astype(jnp.float32)
    cs = jnp.cumsum(uf, axis=1)
    count = jnp.arange(1, s + 1, dtype=jnp.float32)[None, :, None]
    outs = []
    for g, w in enumerate(POOL_WINDOWS):
        sl = slice(g * POOL_GROUP, (g + 1) * POOL_GROUP)
        c = cs[..., sl]
        lag = jnp.pad(c[:, :-w], ((0, 0), (w, 0), (0, 0)))
        mean = (c - lag) / jnp.minimum(count, float(w))
        d = (mean - uf[..., sl]).astype(u.dtype)
        outs.append(d @ pool_w[g])
    return jnp.concatenate(outs, axis=-1) * pool_scale


def stick_breaking_attention(q, k, v, q_norm_g, k_norm_g):
    b, s = q.shape[0], q.shape[1]
    q = rmsnorm(q, q_norm_g).transpose(0, 2, 1, 3).astype(jnp.float32)
    k = rmsnorm(k, k_norm_g).transpose(0, 2, 1, 3).astype(jnp.float32)
    vf = v.transpose(0, 2, 1, 3).astype(jnp.float32)
    scale = HEAD_DIM ** -0.5
    outs = []
    for i in range(s // Q_BLOCK):
        lo, hi = i * Q_BLOCK, (i + 1) * Q_BLOCK
        z = jnp.einsum('bhqd,bhkd->bhqk', q[:, :, lo:hi], k[:, :, :hi]) * scale
        mask = jnp.arange(hi)[None, :] < jnp.arange(lo, hi)[:, None]
        log_keep = jnp.where(mask, jax.nn.log_sigmoid(-z), 0.0)
        log_pass = lax.cumsum(log_keep, axis=3, reverse=True) - log_keep
        a = jnp.where(mask, jnp.exp(jax.nn.log_sigmoid(z) + log_pass), 0.0)
        outs.append(jnp.einsum('bhqk,bhkd->bhqd', a, vf[:, :, :hi]))
    o = jnp.concatenate(outs, axis=2).astype(v.dtype)
    return o.transpose(0, 2, 1, 3).reshape(b, s, ATTN_WIDTH)


def moe(x2, router_w, router_b, w_gate_up, b_gate_up, w_down, b_down):
    t, d = x2.shape
    logits = (x2 @ router_w + router_b).astype(jnp.float32)
    top_v, top_e = lax.top_k(logits, TOP_K)
    gates = jax.nn.softmax(top_v, axis=-1)
    n = t * TOP_K
    flat_e = top_e.reshape(-1)
    flat_g = gates.reshape(-1)
    flat_tok = jnp.arange(n, dtype=jnp.int32) // TOP_K
    order = jnp.argsort(flat_e)
    se = flat_e[order]
    counts = jnp.bincount(flat_e, length=N_EXPERTS)
    starts = jnp.cumsum(counts) - counts
    pcounts = (counts + EXPERT_BLOCK - 1) // EXPERT_BLOCK * EXPERT_BLOCK
    pends = jnp.cumsum(pcounts)
    pstarts = pends - pcounts
    dest = pstarts[se] + jnp.arange(n, dtype=jnp.int32) - starts[se]
    nb = -(-n // EXPERT_BLOCK) + N_EXPERTS
    rows = nb * EXPERT_BLOCK
    buf_tok = jnp.full((rows,), t, dtype=jnp.int32).at[dest].set(flat_tok[order])
    buf_g = jnp.zeros((rows,), jnp.float32).at[dest].set(flat_g[order])
    block_e = jnp.minimum(jnp.searchsorted(pends, jnp.arange(nb, dtype=jnp.int32) * EXPERT_BLOCK, side='right'), N_EXPERTS - 1)
    x_pad = jnp.concatenate([x2, jnp.zeros((1, d), x2.dtype)], axis=0)

    def expert_block(args):
        tok, g, e = args
        xb = x_pad[tok]
        hgu = xb @ w_gate_up[e] + b_gate_up[e]
        gate = jnp.minimum(hgu[:, 0::2], SWIGLU_LIMIT)
        up = jnp.clip(hgu[:, 1::2], -SWIGLU_LIMIT, SWIGLU_LIMIT)
        act = gate * jax.nn.sigmoid(SWIGLU_ALPHA * gate) * (up + 1.0)
        out = act @ w_down[e] + b_down[e]
        return out * g[:, None].astype(out.dtype)

    out = lax.map(expert_block, (buf_tok.reshape(nb, EXPERT_BLOCK), buf_g.reshape(nb, EXPERT_BLOCK), block_e))
    y = jnp.zeros((t + 1, d), out.dtype).at[buf_tok].add(out.reshape(rows, d))
    return y[:t]


def setup_inputs(seed: int = 0) -> dict:
    key = jax.random.key(seed)
    ks = jax.random.split(key, 20)
    f32 = jnp.float32
    nrm = lambda k, shape, s: jax.random.normal(k, shape, f32) * s
    L, D, F = DEPTH, D_MODEL, D_EXPERT
    return {
        'x': nrm(ks[0], (BATCH, SEQ, D), 1.0),
        'p': nrm(ks[1], (DEPTH, BATCH, SEQ, PLE_DIM), 1.0),
        'ln_mix_g': 1.0 + nrm(ks[2], (L, D), 0.05),
        'w_in': nrm(ks[3], (L, D, POOL_WIDTH + 3 * ATTN_WIDTH), D ** -0.5),
        'pool_w': nrm(ks[4], (L, len(POOL_WINDOWS), POOL_GROUP, POOL_GROUP), POOL_GROUP ** -0.5),
        'pool_scale': 1.0 + nrm(ks[5], (L, POOL_WIDTH), 0.1),
        'q_norm_g': 1.0 + nrm(ks[6], (L, HEAD_DIM), 0.05),
        'k_norm_g': 1.0 + nrm(ks[7], (L, HEAD_DIM), 0.05),
        'w_out': nrm(ks[8], (L, MIX_WIDTH, D), MIX_WIDTH ** -0.5),
        'ln_ffn_g': 1.0 + nrm(ks[9], (L, D), 0.05),
        'router_w': nrm(ks[10], (L, D, N_EXPERTS), D ** -0.5),
        'router_b': nrm(ks[11], (L, N_EXPERTS), 0.01),
        'w_gate_up': nrm(ks[12], (L, N_EXPERTS, D, 2 * F), D ** -0.5),
        'b_gate_up': nrm(ks[13], (L, N_EXPERTS, 2 * F), 0.02),
        'w_down': nrm(ks[14], (L, N_EXPERTS, F, D), F ** -0.5),
        'b_down': nrm(ks[15], (L, N_EXPERTS, D), 0.02),
        'ln_ple_g': 1.0 + nrm(ks[16], (L, D), 0.05),
        'ple_gate_w': nrm(ks[17], (L, D, D), D ** -0.5),
        'ple_proj': nrm(ks[18], (L, PLE_DIM, D), PLE_DIM ** -0.5),
        'ple_norm_g': 1.0 + nrm(ks[19], (L, D), 0.05),
    }


def reference(x, p, ln_mix_g, w_in, pool_w, pool_scale, q_norm_g, k_norm_g, w_out, ln_ffn_g,
              router_w, router_b, w_gate_up, b_gate_up, w_down, b_down, ln_ple_g, ple_gate_w,
              ple_proj, ple_norm_g):
    b, s, d = x.shape
    h = x
    for i in range(DEPTH):
        hn = rmsnorm(h, ln_mix_g[i])
        proj = hn @ w_in[i]
        u = proj[..., :POOL_WIDTH]
        q, k, v = jnp.split(proj[..., POOL_WIDTH:], 3, axis=-1)
        q = q.reshape(b, s, N_HEADS, HEAD_DIM)
        k = k.reshape(b, s, N_HEADS, HEAD_DIM)
        v = v.reshape(b, s, N_HEADS, HEAD_DIM)
        pool_out = pool_mixer(u, pool_w[i], pool_scale[i])
        attn_out = stick_breaking_attention(q, k, v, q_norm_g[i], k_norm_g[i])
        h = h + jnp.concatenate([pool_out, attn_out], axis=-1) @ w_out[i]
        hn = rmsnorm(h, ln_ffn_g[i])
        h = h + moe(hn.reshape(b * s, d), router_w[i], router_b[i], w_gate_up[i], b_gate_up[i],
                    w_down[i], b_down[i]).reshape(b, s, d)
        gate = jax.nn.sigmoid(rmsnorm(h, ln_ple_g[i]) @ ple_gate_w[i])
        e = rmsnorm(p[i] @ ple_proj[i], ple_norm_g[i])
        h = h + gate * e
    return h
```

```python
import functools

import jax
import jax.numpy as jnp
from jax import lax
from jax.experimental import pallas as pl
from jax.experimental.pallas import tpu as pltpu

F32 = jnp.float32
BF16 = jnp.bfloat16
I32 = jnp.int32

HEAD_DIM = 64
POOL_WINDOWS = (2, 4, 8, 16)
TOP_K = 4
SWIGLU_LIMIT = 7.0
SWIGLU_ALPHA = 1.702
EPS = 1e-6

LANES = 128
VMEM_LIMIT = 56 * 1024 * 1024
UNIT_ROWS = 1024
SUB_ROWS = 256


def _params(*sem):
    return pltpu.CompilerParams(dimension_semantics=sem, vmem_limit_bytes=VMEM_LIMIT)


def _rms(x, g):
    return x * lax.rsqrt(jnp.mean(x * x, axis=-1, keepdims=True) + EPS) * g


def _split_bf16(x):
    hi = x.astype(BF16)
    lo = (x - hi.astype(F32)).astype(BF16)
    return hi, lo


def _in_proj_kernel(x_ref, g_ref, w_ref, o_ref, xn_ref):
    @pl.when(pl.program_id(1) == 0)
    def _():
        xn_ref[...] = _rms(x_ref[...], g_ref[...]).astype(BF16)

    o_ref[...] = jnp.dot(xn_ref[...], w_ref[...], preferred_element_type=F32)


def _in_proj(x2, g, w, *, tm, tn):
    t, d = x2.shape
    n = w.shape[1]
    return pl.pallas_call(
        _in_proj_kernel,
        grid=(t // tm, n // tn),
        in_specs=[
            pl.BlockSpec((tm, d), lambda i, j: (i, 0)),
            pl.BlockSpec((1, d), lambda i, j: (0, 0)),
            pl.BlockSpec((d, tn), lambda i, j: (0, j)),
        ],
        out_specs=pl.BlockSpec((tm, tn), lambda i, j: (i, j)),
        out_shape=jax.ShapeDtypeStruct((t, n), F32),
        scratch_shapes=[pltpu.VMEM((tm, d), BF16)],
        compiler_params=_params("parallel", "arbitrary"),
        name="in_proj",
    )(x2, g, w)


def _pool_kernel(u_ref, w_ref, s_ref, o_ref, *, group):
    seq = u_ref.shape[1]
    row = lax.broadcasted_iota(I32, (seq, 1), 0)
    for gi, win in enumerate(POOL_WINDOWS):
        sl = slice(gi * group, (gi + 1) * group)
        u = u_ref[0, :, sl]
        acc = u
        span = 1
        while span < win:
            shifted = pltpu.roll(acc, span, axis=0)
            acc = acc + jnp.where(row >= span, shifted, 0.0)
            span *= 2
        count = jnp.minimum(row + 1, win).astype(F32)
        d = (acc / count - u).astype(BF16)
        y = jnp.dot(d, w_ref[gi], preferred_element_type=F32)
        o_ref[0, :, sl] = (y * s_ref[:, sl]).astype(BF16)


def _pool_mixer(proj3, pool_w, pool_scale, *, width):
    b, s, _ = proj3.shape
    group = width // len(POOL_WINDOWS)
    return pl.pallas_call(
        functools.partial(_pool_kernel, group=group),
        grid=(b,),
        in_specs=[
            pl.BlockSpec((1, s, width), lambda i: (i, 0, 0)),
            pl.BlockSpec((len(POOL_WINDOWS), group, group), lambda i: (0, 0, 0)),
            pl.BlockSpec((1, width), lambda i: (0, 0)),
        ],
        out_specs=pl.BlockSpec((1, s, width), lambda i: (i, 0, 0)),
        out_shape=jax.ShapeDtypeStruct((b, s, width), BF16),
        compiler_params=_params("parallel"),
        name="pool_mixer",
    )(proj3, pool_w, pool_scale)


def _head_rmsnorm(x, g2, bd):
    hi, lo = _split_bf16(x * x)
    ss = jnp.dot(hi, bd, preferred_element_type=F32) + jnp.dot(lo, bd, preferred_element_type=F32)
    return x * lax.rsqrt(ss * (1.0 / HEAD_DIM) + EPS) * g2


def _stickbreak_kernel(q_ref, k_ref, v_ref, qg_ref, kg_ref, bd_ref, tri_ref, o_ref, kn_ref, vb_ref, *, blk):
    qi = pl.program_id(2)
    lane = lax.broadcasted_iota(I32, (1, LANES), 1)
    head0 = lane < HEAD_DIM

    @pl.when(qi == 0)
    def _():
        kn_ref[...] = _head_rmsnorm(k_ref[0], kg_ref[...], bd_ref[...]).astype(BF16)
        vb_ref[...] = v_ref[0].astype(BF16)

    q = _head_rmsnorm(q_ref[0], qg_ref[...], bd_ref[...]) * (HEAD_DIM ** -0.5)
    q2 = jnp.concatenate([jnp.where(head0, q, 0.0), jnp.where(head0, 0.0, q)], axis=0).astype(BF16)
    tri = tri_ref[...]

    def tile(j, carry, acc, masked):
        ks = pl.multiple_of(j * blk, blk)
        kc = kn_ref[pl.ds(ks, blk), :]
        vc = vb_ref[pl.ds(ks, blk), :]
        z = lax.dot_general(q2, kc, (((1,), (1,)), ((), ())), preferred_element_type=F32)
        sp = jnp.maximum(z, 0.0) + jnp.log(1.0 + jnp.exp(-jnp.abs(z)))
        if masked:
            r = lax.broadcasted_iota(I32, (blk, blk), 0)
            c = lax.broadcasted_iota(I32, (blk, blk), 1)
            m1 = c < r
            mask = jnp.concatenate([m1, m1], axis=0)
            sp = jnp.where(mask, sp, 0.0)
        hi, lo = _split_bf16(sp)
        csum = jnp.dot(jnp.concatenate([hi, lo], axis=1), tri, preferred_element_type=F32)
        a = jnp.exp(z - csum - carry)
        if masked:
            a = jnp.where(mask, a, 0.0)
        acc = acc + jnp.dot(a.astype(BF16), vc, preferred_element_type=F32)
        carry = carry + csum[:, 0:1]
        return carry, acc

    carry0 = jnp.zeros((2 * blk, 1), F32)
    acc0 = jnp.zeros((2 * blk, LANES), F32)
    carry, acc = tile(qi, carry0, acc0, True)

    def body(step, state):
        return tile(qi - 1 - step, state[0], state[1], False)

    carry, acc = lax.fori_loop(0, qi, body, (carry, acc))
    o_ref[0] = jnp.where(head0, acc[:blk], acc[blk:]).astype(BF16)


def _stickbreak(proj3, q_norm_g, k_norm_g, *, pool_width, attn_width, blk):
    b, s, _ = proj3.shape
    pairs = attn_width // LANES
    qoff = pool_width // LANES
    koff = qoff + pairs
    voff = koff + pairs
    qg2 = jnp.concatenate([q_norm_g, q_norm_g]).reshape(1, LANES).astype(F32)
    kg2 = jnp.concatenate([k_norm_g, k_norm_g]).reshape(1, LANES).astype(F32)
    li = jnp.arange(LANES) // HEAD_DIM
    bd = (li[:, None] == li[None, :]).astype(BF16)
    tr = (jnp.arange(blk)[:, None] >= jnp.arange(blk)[None, :]).astype(BF16)
    tri = jnp.concatenate([tr, tr], axis=0)
    return pl.pallas_call(
        functools.partial(_stickbreak_kernel, blk=blk),
        grid=(b, pairs, s // blk),
        in_specs=[
            pl.BlockSpec((1, blk, LANES), lambda i, h, j: (i, j, qoff + h)),
            pl.BlockSpec((1, s, LANES), lambda i, h, j: (i, 0, koff + h)),
            pl.BlockSpec((1, s, LANES), lambda i, h, j: (i, 0, voff + h)),
            pl.BlockSpec((1, LANES), lambda i, h, j: (0, 0)),
            pl.BlockSpec((1, LANES), lambda i, h, j: (0, 0)),
            pl.BlockSpec((LANES, LANES), lambda i, h, j: (0, 0)),
            pl.BlockSpec((2 * blk, blk), lambda i, h, j: (0, 0)),
        ],
        out_specs=pl.BlockSpec((1, blk, LANES), lambda i, h, j: (i, j, h)),
        out_shape=jax.ShapeDtypeStruct((b, s, attn_width), BF16),
        scratch_shapes=[pltpu.VMEM((s, LANES), BF16), pltpu.VMEM((s, LANES), BF16)],
        compiler_params=_params("parallel", "parallel", "arbitrary"),
        name="stickbreak",
    )(proj3, proj3, proj3, qg2, kg2, bd, tri)


def _mix_router_kernel(pool_ref, attn_ref, x_ref, wp_ref, wa_ref, g_ref, rw_ref, rb_ref, lt_ref,
                       h_ref, hn_ref, te_ref, tg_ref, rk_ref, cnt_ref, run_ref, *, n_exp):
    i = pl.program_id(0)

    @pl.when(i == 0)
    def _():
        run_ref[...] = jnp.zeros_like(run_ref)

    h = (x_ref[...]
         + jnp.dot(pool_ref[...], wp_ref[...], preferred_element_type=F32)
         + jnp.dot(attn_ref[...], wa_ref[...], preferred_element_type=F32))
    h_ref[...] = h
    hn = _rms(h, g_ref[...])
    hn_ref[...] = hn
    logits = lax.dot_general(rw_ref[...], hn.astype(BF16), (((1,), (1,)), ((), ())),
                             preferred_element_type=F32) + rb_ref[...]
    tm = logits.shape[1]
    eidx = lax.broadcasted_iota(I32, (n_exp, tm), 0).astype(F32)
    vals, idxs, hots = [], [], []
    rem = logits
    for _ in range(TOP_K):
        m = jnp.max(rem, axis=0, keepdims=True)
        first = jnp.min(jnp.where(rem == m, eidx, float(n_exp)), axis=0, keepdims=True)
        hot = eidx == first
        vals.append(m)
        idxs.append(first)
        hots.append(hot)
        rem = jnp.where(hot, -jnp.inf, rem)
    ex = [jnp.exp(v - vals[0]) for v in vals]
    den = ex[0] + ex[1] + ex[2] + ex[3]
    tg_ref[...] = jnp.concatenate([e / den for e in ex], axis=0)
    te_ref[...] = jnp.concatenate(idxs, axis=0).astype(I32)
    hot_all = (hots[0] | hots[1] | hots[2] | hots[3]).astype(F32)
    before = jnp.dot(hot_all.astype(BF16), lt_ref[...], preferred_element_type=F32)
    base = run_ref[:, 0:1] + before
    rk_ref[...] = jnp.concatenate(
        [jnp.sum(jnp.where(hot, base, 0.0), axis=0, keepdims=True) for hot in hots], axis=0).astype(I32)
    run_ref[...] = run_ref[...] + jnp.sum(hot_all, axis=1, keepdims=True)
    cnt_ref[...] = run_ref[...].astype(I32)


def _mix_router(pool2, attn2, x2, w_pool, w_attn, g, rw_t, rb, *, tm):
    t, d = x2.shape
    half = pool2.shape[1]
    n_exp = rw_t.shape[0]
    lt = (jnp.arange(tm)[:, None] < jnp.arange(tm)[None, :]).astype(BF16)
    row = lambda i: (i, 0)
    fix = lambda i: (0, 0)
    col = lambda i: (0, i)
    return pl.pallas_call(
        functools.partial(_mix_router_kernel, n_exp=n_exp),
        grid=(t // tm,),
        in_specs=[
            pl.BlockSpec((tm, half), row),
            pl.BlockSpec((tm, half), row),
            pl.BlockSpec((tm, d), row),
            pl.BlockSpec((half, d), fix),
            pl.BlockSpec((half, d), fix),
            pl.BlockSpec((1, d), fix),
            pl.BlockSpec((n_exp, d), fix),
            pl.BlockSpec((n_exp, 1), fix),
            pl.BlockSpec((tm, tm), fix),
        ],
        out_specs=[
            pl.BlockSpec((tm, d), row),
            pl.BlockSpec((tm, d), row),
            pl.BlockSpec((TOP_K, tm), col),
            pl.BlockSpec((TOP_K, tm), col),
            pl.BlockSpec((TOP_K, tm), col),
            pl.BlockSpec((n_exp, LANES), fix),
        ],
        out_shape=[
            jax.ShapeDtypeStruct((t, d), F32),
            jax.ShapeDtypeStruct((t, d), F32),
            jax.ShapeDtypeStruct((TOP_K, t), I32),
            jax.ShapeDtypeStruct((TOP_K, t), F32),
            jax.ShapeDtypeStruct((TOP_K, t), I32),
            jax.ShapeDtypeStruct((n_exp, LANES), I32),
        ],
        scratch_shapes=[pltpu.VMEM((n_exp, LANES), F32)],
        compiler_params=_params("arbitrary"),
        name="mix_router",
    )(pool2, attn2, x2, w_pool, w_attn, g, rw_t, rb, lt)


def _moe_kernel(ue_ref, un_ref, tok_ref, dst_ref, hn_hbm, wg_ref, wu_ref, wd_ref, bg_ref, bu_ref, bd_ref,
                out_hbm, xf_ref, xb_ref, acc_ref, gsem, ssem):
    u = pl.program_id(0)
    f = pl.program_id(1)
    nf = pl.num_programs(1)
    nsub = (un_ref[u] + (SUB_ROWS - 1)) // SUB_ROWS

    def gather_row(r):
        return pltpu.make_async_copy(hn_hbm.at[pl.ds(tok_ref[r], 1)], xf_ref.at[pl.ds(r, 1)],
                                     gsem.at[r // SUB_ROWS])

    def scatter_row(r):
        return pltpu.make_async_copy(acc_ref.at[pl.ds(r, 1)], out_hbm.at[pl.ds(dst_ref[r], 1)],
                                     ssem.at[r // SUB_ROWS])

    @pl.when(jnp.logical_and(f == 0, nsub > 0))
    def _():
        def start(r, c):
            gather_row(r).start()
            return c

        lax.fori_loop(0, nsub * SUB_ROWS, start, 0)

        def wait(s, c):
            r0 = pl.multiple_of(s * SUB_ROWS, SUB_ROWS)
            pltpu.make_async_copy(hn_hbm.at[pl.ds(0, SUB_ROWS)], xf_ref.at[pl.ds(r0, SUB_ROWS)],
                                  gsem.at[s]).wait()
            xb_ref[pl.ds(r0, SUB_ROWS), :] = xf_ref[pl.ds(r0, SUB_ROWS), :].astype(BF16)
            return c

        lax.fori_loop(0, nsub, wait, 0)

    def sub(s, c):
        r0 = pl.multiple_of(s * SUB_ROWS, SUB_ROWS)
        xs = xb_ref[pl.ds(r0, SUB_ROWS), :]
        gate = jnp.dot(xs, wg_ref[0], preferred_element_type=F32) + bg_ref[0]
        up = jnp.dot(xs, wu_ref[0], preferred_element_type=F32) + bu_ref[0]
        gate = jnp.minimum(gate, SWIGLU_LIMIT)
        up = jnp.clip(up, -SWIGLU_LIMIT, SWIGLU_LIMIT)
        act = gate * jax.nn.sigmoid(SWIGLU_ALPHA * gate) * (up + 1.0)
        y = jnp.dot(act.astype(BF16), wd_ref[0], preferred_element_type=F32)

        @pl.when(f == 0)
        def _():
            acc_ref[pl.ds(r0, SUB_ROWS), :] = y + bd_ref[0]

        @pl.when(f > 0)
        def _():
            acc_ref[pl.ds(r0, SUB_ROWS), :] += y

        return c

    lax.fori_loop(0, nsub, sub, 0)

    @pl.when(jnp.logical_and(f == nf - 1, nsub > 0))
    def _():
        def start(r, c):
            scatter_row(r).start()
            return c

        nvalid = un_ref[u]
        lax.fori_loop(0, nvalid, start, 0)
        nfull = nvalid // SUB_ROWS

        def wait(s, c):
            r0 = pl.multiple_of(s * SUB_ROWS, SUB_ROWS)
            pltpu.make_async_copy(acc_ref.at[pl.ds(r0, SUB_ROWS)], out_hbm.at[pl.ds(0, SUB_ROWS)],
                                  ssem.at[s]).wait()
            return c

        lax.fori_loop(0, nfull, wait, 0)

        def wait_row(r, c):
            scatter_row(r).wait()
            return c

        lax.fori_loop(nfull * SUB_ROWS, nvalid, wait_row, 0)


def _moe_experts(hn, unit_e, unit_n, row_tok, row_dst, wg, wu, wd, bg, bu, bd, *, n_slots, tf):
    t, d = hn.shape
    n_units = unit_e.shape[0]
    hidden = wg.shape[2]
    nf = hidden // tf
    grid_spec = pltpu.PrefetchScalarGridSpec(
        num_scalar_prefetch=2,
        grid=(n_units, nf),
        in_specs=[
            pl.BlockSpec((UNIT_ROWS,), lambda u, f, ue, un: (u,), memory_space=pltpu.SMEM),
            pl.BlockSpec((UNIT_ROWS,), lambda u, f, ue, un: (u,), memory_space=pltpu.SMEM),
            pl.BlockSpec(memory_space=pl.ANY),
            pl.BlockSpec((1, d, tf), lambda u, f, ue, un: (ue[u], 0, f)),
            pl.BlockSpec((1, d, tf), lambda u, f, ue, un: (ue[u], 0, f)),
            pl.BlockSpec((1, tf, d), lambda u, f, ue, un: (ue[u], f, 0)),
            pl.BlockSpec((1, 1, tf), lambda u, f, ue, un: (ue[u], 0, f)),
            pl.BlockSpec((1, 1, tf), lambda u, f, ue, un: (ue[u], 0, f)),
            pl.BlockSpec((1, 1, d), lambda u, f, ue, un: (ue[u], 0, 0)),
        ],
        out_specs=pl.BlockSpec(memory_space=pl.ANY),
        scratch_shapes=[
            pltpu.VMEM((UNIT_ROWS, d), F32),
            pltpu.VMEM((UNIT_ROWS, d), BF16),
            pltpu.VMEM((UNIT_ROWS, d), F32),
            pltpu.SemaphoreType.DMA((UNIT_ROWS // SUB_ROWS,)),
            pltpu.SemaphoreType.DMA((UNIT_ROWS // SUB_ROWS,)),
        ],
    )
    return pl.pallas_call(
        _moe_kernel,
        grid_spec=grid_spec,
        out_shape=jax.ShapeDtypeStruct((n_slots, d), F32),
        compiler_params=_params("arbitrary", "arbitrary"),
        name="moe_experts",
    )(unit_e, unit_n, row_tok, row_dst, hn, wg, wu, wd, bg, bu, bd)


def _ple_kernel(h_ref, y_ref, tg_ref, p_ref, g1_ref, wg_ref, wp_ref, g2_ref, o_ref):
    d = h_ref.shape[1]
    h = h_ref[...]
    for k in range(TOP_K):
        h = h + y_ref[:, k * d:(k + 1) * d] * tg_ref[:, k:k + 1]
    gate = jax.nn.sigmoid(jnp.dot(_rms(h, g1_ref[...]).astype(BF16), wg_ref[...], preferred_element_type=F32))
    e = _rms(jnp.dot(p_ref[...].astype(BF16), wp_ref[...], preferred_element_type=F32), g2_ref[...])
    o_ref[...] = h + gate * e


def _ple_out(h1, slots2, gates_t, p2, g1, w_gate, w_proj, g2, *, tm):
    t, d = h1.shape
    pd = p2.shape[1]
    row = lambda i: (i, 0)
    fix = lambda i: (0, 0)
    return pl.pallas_call(
        _ple_kernel,
        grid=(t // tm,),
        in_specs=[
            pl.BlockSpec((tm, d), row),
            pl.BlockSpec((tm, TOP_K * d), row),
            pl.BlockSpec((tm, TOP_K), row),
            pl.BlockSpec((tm, pd), row),
            pl.BlockSpec((1, d), fix),
            pl.BlockSpec((d, d), fix),
            pl.BlockSpec((pd, d), fix),
            pl.BlockSpec((1, d), fix),
        ],
        out_specs=pl.BlockSpec((tm, d), row),
        out_shape=jax.ShapeDtypeStruct((t, d), F32),
        compiler_params=_params("parallel"),
        name="ple_out",
    )(h1, slots2, gates_t, p2, g1, w_gate, w_proj, g2)


def _routing_tables(top_e, rank, counts, *, n_units, n_tok):
    n_exp = counts.shape[0]
    units_per = (counts + (UNIT_ROWS - 1)) // UNIT_ROWS
    unit_end = jnp.cumsum(units_per)
    unit_start = unit_end - units_per
    uid = jnp.arange(n_units, dtype=I32)
    unit_e = jnp.minimum(jnp.searchsorted(unit_end, uid, side="right"), n_exp - 1).astype(I32)
    unit_n = jnp.clip(counts[unit_e] - (uid - unit_start[unit_e]) * UNIT_ROWS, 0, UNIT_ROWS).astype(I32)
    row = (unit_start[top_e] + rank // UNIT_ROWS) * UNIT_ROWS + rank % UNIT_ROWS
    slot = jnp.arange(n_tok, dtype=I32)[None, :] * TOP_K + jnp.arange(TOP_K, dtype=I32)[:, None]
    n_rows = n_units * UNIT_ROWS
    row_slot = jnp.full((n_rows,), -1, I32).at[row.reshape(-1)].set(slot.reshape(-1), unique_indices=True)
    row_dst = jnp.maximum(row_slot, 0)
    return unit_e, unit_n, row_dst // TOP_K, row_dst


def _tile(n, want):
    want = min(n, want)
    assert n % want == 0, (n, want)
    return want


def _layer(h, p_i, ln_mix_g, w_in, pool_w, pool_scale, q_norm_g, k_norm_g, w_out, ln_ffn_g, router_w, router_b,
           w_gate_up, b_gate_up, w_down, b_down, ln_ple_g, ple_gate_w, ple_proj, ple_norm_g):
    b, s, d = h.shape
    t = b * s
    pool_width = d // 2
    attn_width = d // 2
    n_exp = router_w.shape[1]
    hidden = w_down.shape[1]
    assert pool_width % (LANES * len(POOL_WINDOWS)) == 0 and attn_width % LANES == 0
    x2 = h.reshape(t, d)

    proj = _in_proj(x2, ln_mix_g.reshape(1, d), w_in.astype(BF16), tm=_tile(t, 1024), tn=_tile(w_in.shape[1], 1024))
    proj3 = proj.reshape(b, s, -1)
    pool_out = _pool_mixer(proj3, pool_w.astype(BF16), pool_scale.reshape(1, pool_width), width=pool_width)
    attn_out = _stickbreak(proj3, q_norm_g, k_norm_g, pool_width=pool_width, attn_width=attn_width,
                           blk=_tile(s, 256))

    w_out_b = w_out.astype(BF16)
    h1, hn, top_e, gates, rank, counts = _mix_router(
        pool_out.reshape(t, pool_width), attn_out.reshape(t, attn_width), x2,
        w_out_b[:pool_width], w_out_b[pool_width:], ln_ffn_g.reshape(1, d),
        router_w.T.astype(BF16), router_b.reshape(n_exp, 1), tm=_tile(t, 512))

    n_units = t * TOP_K // UNIT_ROWS + n_exp
    unit_e, unit_n, row_tok, row_dst = _routing_tables(top_e, rank, counts[:, 0], n_units=n_units, n_tok=t)
    wg = w_gate_up[:, :, 0::2].astype(BF16)
    wu = w_gate_up[:, :, 1::2].astype(BF16)
    bg = b_gate_up[:, 0::2].reshape(n_exp, 1, hidden)
    bu = b_gate_up[:, 1::2].reshape(n_exp, 1, hidden)
    slots = _moe_experts(hn, unit_e, unit_n, row_tok, row_dst, wg, wu, w_down.astype(BF16), bg, bu,
                         b_down.reshape(n_exp, 1, d), n_slots=t * TOP_K, tf=_tile(hidden, 512))

    out = _ple_out(h1, slots.reshape(t, TOP_K * d), gates.T, p_i.reshape(t, -1),
                   ln_ple_g.reshape(1, d), ple_gate_w.astype(BF16), ple_proj.astype(BF16),
                   ple_norm_g.reshape(1, d), tm=_tile(t, 256))
    return out.reshape(b, s, d)


def kernel(x, p, ln_mix_g, w_in, pool_w, pool_scale, q_norm_g, k_norm_g, w_out, ln_ffn_g, router_w, router_b,
           w_gate_up, b_gate_up, w_down, b_down, ln_ple_g, ple_gate_w, ple_proj, ple_norm_g):
    h = x
    for i in range(p.shape[0]):
        h = _layer(h, p[i], ln_mix_g[i], w_in[i], pool_w[i], pool_scale[i], q_norm_g[i], k_norm_g[i], w_out[i],
                   ln_ffn_g[i], router_w[i], router_b[i], w_gate_up[i], b_gate_up[i], w_down[i], b_down[i],
                   ln_ple_g[i], ple_gate_w[i], ple_proj[i], ple_norm_g[i])
    return h
```

```python
import functools

import jax
import jax.numpy as jnp
from jax import lax
from jax.experimental import pallas as pl
from jax.experimental.pallas import tpu as pltpu

F32 = jnp.float32
BF16 = jnp.bfloat16
I32 = jnp.int32

HEAD_DIM = 64
POOL_WINDOWS = (2, 4, 8, 16)
TOP_K = 4
SWIGLU_LIMIT = 7.0
SWIGLU_ALPHA = 1.702
EPS = 1e-6

LANES = 128
VMEM_LIMIT = 56 * 1024 * 1024
UNIT_ROWS = 1024
SUB_ROWS = 256


def _params(*sem):
    return pltpu.CompilerParams(dimension_semantics=sem, vmem_limit_bytes=VMEM_LIMIT)


def _rms(x, g):
    return x * lax.rsqrt(jnp.mean(x * x, axis=-1, keepdims=True) + EPS) * g


def _split_bf16(x):
    hi = x.astype(BF16)
    lo = (x - hi.astype(F32)).astype(BF16)
    return hi, lo


def _in_proj_kernel(x_ref, g_ref, w_ref, o_ref, xn_ref):
    @pl.when(pl.program_id(1) == 0)
    def _():
        xn_ref[...] = _rms(x_ref[...], g_ref[...]).astype(BF16)

    o_ref[...] = jnp.dot(xn_ref[...], w_ref[...], preferred_element_type=F32)


def _in_proj(x2, g, w, *, tm, tn):
    t, d = x2.shape
    n = w.shape[1]
    return pl.pallas_call(
        _in_proj_kernel,
        grid=(t // tm, n // tn),
        in_specs=[
            pl.BlockSpec((tm, d), lambda i, j: (i, 0)),
            pl.BlockSpec((1, d), lambda i, j: (0, 0)),
            pl.BlockSpec((d, tn), lambda i, j: (0, j)),
        ],
        out_specs=pl.BlockSpec((tm, tn), lambda i, j: (i, j)),
        out_shape=jax.ShapeDtypeStruct((t, n), F32),
        scratch_shapes=[pltpu.VMEM((tm, d), BF16)],
        compiler_params=_params("parallel", "arbitrary"),
        name="in_proj",
    )(x2, g, w)


def _pool_kernel(u_ref, w_ref, s_ref, o_ref, *, group):
    seq = u_ref.shape[1]
    row = lax.broadcasted_iota(I32, (seq, 1), 0)
    for gi, win in enumerate(POOL_WINDOWS):
        sl = slice(gi * group, (gi + 1) * group)
        u = u_ref[0, :, sl]
        acc = u
        span = 1
        while span < win:
            shifted = pltpu.roll(acc, span, axis=0)
            acc = acc + jnp.where(row >= span, shifted, 0.0)
            span *= 2
        count = jnp.minimum(row + 1, win).astype(F32)
        d = (acc / count - u).astype(BF16)
        y = jnp.dot(d, w_ref[gi], preferred_element_type=F32)
        o_ref[0, :, sl] = (y * s_ref[:, sl]).astype(BF16)


def _pool_mixer(proj3, pool_w, pool_scale, *, width):
    b, s, _ = proj3.shape
    group = width // len(POOL_WINDOWS)
    return pl.pallas_call(
        functools.partial(_pool_kernel, group=group),
        grid=(b,),
        in_specs=[
            pl.BlockSpec((1, s, width), lambda i: (i, 0, 0)),
            pl.BlockSpec((len(POOL_WINDOWS), group, group), lambda i: (0, 0, 0)),
            pl.BlockSpec((1, width), lambda i: (0, 0)),
        ],
        out_specs=pl.BlockSpec((1, s, width), lambda i: (i, 0, 0)),
        out_shape=jax.ShapeDtypeStruct((b, s, width), BF16),
        compiler_params=_params("parallel"),
        name="pool_mixer",
    )(proj3, pool_w, pool_scale)


def _head_rmsnorm(x, g2, bd):
    hi, lo = _split_bf16(x * x)
    ss = jnp.dot(hi, bd, preferred_element_type=F32) + jnp.dot(lo, bd, preferred_element_type=F32)
    return x * lax.rsqrt(ss * (1.0 / HEAD_DIM) + EPS) * g2


def _stickbreak_kernel(q_ref, k_ref, v_ref, qg_ref, kg_ref, bd_ref, tri_ref, o_ref,
                       kn_ref, vb_ref, q2_ref, carry_ref, acc_ref, *, blk):
    qi = pl.program_id(2)
    lane = lax.broadcasted_iota(I32, (1, LANES), 1)
    head0 = lane < HEAD_DIM

    @pl.when(qi == 0)
    def _():
        kn_ref[...] = _head_rmsnorm(k_ref[0], kg_ref[...], bd_ref[...]).astype(BF16)
        vb_ref[...] = v_ref[0].astype(BF16)

    q = _head_rmsnorm(q_ref[0], qg_ref[...], bd_ref[...]) * (HEAD_DIM ** -0.5)
    q2_ref[...] = jnp.concatenate([jnp.where(head0, q, 0.0), jnp.where(head0, 0.0, q)], axis=0).astype(BF16)
    carry_ref[...] = jnp.zeros_like(carry_ref)
    acc_ref[...] = jnp.zeros_like(acc_ref)

    def tile(j, masked):
        ks = pl.multiple_of(j * blk, blk)
        z = lax.dot_general(q2_ref[...], kn_ref[pl.ds(ks, blk), :], (((1,), (1,)), ((), ())),
                            preferred_element_type=F32)
        sp = jnp.maximum(z, 0.0) + jnp.log(1.0 + jnp.exp(-jnp.abs(z)))
        if masked:
            r = lax.broadcasted_iota(I32, (2 * blk, blk), 0) & (blk - 1)
            c = lax.broadcasted_iota(I32, (2 * blk, blk), 1)
            keep = c < r
            sp = jnp.where(keep, sp, 0.0)
            z = jnp.where(keep, z, -1e30)
        hi, lo = _split_bf16(sp)
        csum = jnp.dot(jnp.concatenate([hi, lo], axis=1), tri_ref[...], preferred_element_type=F32)
        carry = carry_ref[...]
        a = jnp.concatenate(
            [jnp.exp(z[:, h * LANES:(h + 1) * LANES] - csum[:, h * LANES:(h + 1) * LANES] - carry)
             for h in range(blk // LANES)], axis=1)
        acc_ref[...] += jnp.dot(a.astype(BF16), vb_ref[pl.ds(ks, blk), :], preferred_element_type=F32)
        carry_ref[...] = carry + jnp.broadcast_to(csum[:, 0:1], carry.shape)

    tile(qi, True)

    def body(step, c):
        tile(qi - 1 - step, False)
        return c

    lax.fori_loop(0, qi, body, 0)
    o_ref[0] = jnp.where(head0, acc_ref[:blk], acc_ref[blk:]).astype(BF16)


def _stickbreak(proj3, q_norm_g, k_norm_g, *, pool_width, attn_width, blk):
    b, s, _ = proj3.shape
    pairs = attn_width // LANES
    qoff = pool_width // LANES
    koff = qoff + pairs
    voff = koff + pairs
    qg2 = jnp.concatenate([q_norm_g, q_norm_g]).reshape(1, LANES).astype(F32)
    kg2 = jnp.concatenate([k_norm_g, k_norm_g]).reshape(1, LANES).astype(F32)
    li = jnp.arange(LANES) // HEAD_DIM
    bd = (li[:, None] == li[None, :]).astype(BF16)
    tr = (jnp.arange(blk)[:, None] >= jnp.arange(blk)[None, :]).astype(BF16)
    tri = jnp.concatenate([tr, tr], axis=0)
    return pl.pallas_call(
        functools.partial(_stickbreak_kernel, blk=blk),
        grid=(b, pairs, s // blk),
        in_specs=[
            pl.BlockSpec((1, blk, LANES), lambda i, h, j: (i, j, qoff + h)),
            pl.BlockSpec((1, s, LANES), lambda i, h, j: (i, 0, koff + h)),
            pl.BlockSpec((1, s, LANES), lambda i, h, j: (i, 0, voff + h)),
            pl.BlockSpec((1, LANES), lambda i, h, j: (0, 0)),
            pl.BlockSpec((1, LANES), lambda i, h, j: (0, 0)),
            pl.BlockSpec((LANES, LANES), lambda i, h, j: (0, 0)),
            pl.BlockSpec((2 * blk, blk), lambda i, h, j: (0, 0)),
        ],
        out_specs=pl.BlockSpec((1, blk, LANES), lambda i, h, j: (i, j, h)),
        out_shape=jax.ShapeDtypeStruct((b, s, attn_width), BF16),
        scratch_shapes=[
            pltpu.VMEM((s, LANES), BF16),
            pltpu.VMEM((s, LANES), BF16),
            pltpu.VMEM((2 * blk, LANES), BF16),
            pltpu.VMEM((2 * blk, LANES), F32),
            pltpu.VMEM((2 * blk, LANES), F32),
        ],
        compiler_params=_params("parallel", "parallel", "arbitrary"),
        name="stickbreak",
    )(proj3, proj3, proj3, qg2, kg2, bd, tri)


def _mix_router_kernel(pool_ref, attn_ref, x_ref, wp_ref, wa_ref, g_ref, rw_ref, rb_ref, lt_ref,
                       h_ref, hn_ref, te_ref, tg_ref, rk_ref, cnt_ref, run_ref, *, n_exp):
    i = pl.program_id(0)

    @pl.when(i == 0)
    def _():
        run_ref[...] = jnp.zeros_like(run_ref)

    h = (x_ref[...]
         + jnp.dot(pool_ref[...], wp_ref[...], preferred_element_type=F32)
         + jnp.dot(attn_ref[...], wa_ref[...], preferred_element_type=F32))
    h_ref[...] = h
    hn = _rms(h, g_ref[...])
    hn_ref[...] = hn
    logits = lax.dot_general(rw_ref[...], hn.astype(BF16), (((1,), (1,)), ((), ())),
                             preferred_element_type=F32) + rb_ref[...]
    tm = logits.shape[1]
    eidx = lax.broadcasted_iota(I32, (n_exp, tm), 0).astype(F32)
    vals, idxs, hots = [], [], []
    rem = logits
    for _ in range(TOP_K):
        m = jnp.max(rem, axis=0, keepdims=True)
        first = jnp.min(jnp.where(rem == m, eidx, float(n_exp)), axis=0, keepdims=True)
        hot = eidx == first
        vals.append(m)
        idxs.append(first)
        hots.append(hot)
        rem = jnp.where(hot, -jnp.inf, rem)
    ex = [jnp.exp(v - vals[0]) for v in vals]
    den = ex[0] + ex[1] + ex[2] + ex[3]
    tg_ref[...] = jnp.concatenate([e / den for e in ex], axis=0)
    te_ref[...] = jnp.concatenate(idxs, axis=0).astype(I32)
    hot_all = (hots[0] | hots[1] | hots[2] | hots[3]).astype(F32)
    before = jnp.dot(hot_all.astype(BF16), lt_ref[...], preferred_element_type=F32)
    base = run_ref[:, 0:1] + before
    rk_ref[...] = jnp.concatenate(
        [jnp.sum(jnp.where(hot, base, 0.0), axis=0, keepdims=True) for hot in hots], axis=0).astype(I32)
    run_ref[...] = run_ref[...] + jnp.sum(hot_all, axis=1, keepdims=True)
    cnt_ref[...] = run_ref[...].astype(I32)


def _mix_router(pool2, attn2, x2, w_pool, w_attn, g, rw_t, rb, *, tm):
    t, d = x2.shape
    half = pool2.shape[1]
    n_exp = rw_t.shape[0]
    lt = (jnp.arange(tm)[:, None] < jnp.arange(tm)[None, :]).astype(BF16)
    row = lambda i: (i, 0)
    fix = lambda i: (0, 0)
    col = lambda i: (0, i)
    return pl.pallas_call(
        functools.partial(_mix_router_kernel, n_exp=n_exp),
        grid=(t // tm,),
        in_specs=[
            pl.BlockSpec((tm, half), row),
            pl.BlockSpec((tm, half), row),
            pl.BlockSpec((tm, d), row),
            pl.BlockSpec((half, d), fix),
            pl.BlockSpec((half, d), fix),
            pl.BlockSpec((1, d), fix),
            pl.BlockSpec((n_exp, d), fix),
            pl.BlockSpec((n_exp, 1), fix),
            pl.BlockSpec((tm, tm), fix),
        ],
        out_specs=[
            pl.BlockSpec((tm, d), row),
            pl.BlockSpec((tm, d), row),
            pl.BlockSpec((TOP_K, tm), col),
            pl.BlockSpec((TOP_K, tm), col),
            pl.BlockSpec((TOP_K, tm), col),
            pl.BlockSpec((n_exp, LANES), fix),
        ],
        out_shape=[
            jax.ShapeDtypeStruct((t, d), F32),
            jax.ShapeDtypeStruct((t, d), F32),
            jax.ShapeDtypeStruct((TOP_K, t), I32),
            jax.ShapeDtypeStruct((TOP_K, t), F32),
            jax.ShapeDtypeStruct((TOP_K, t), I32),
            jax.ShapeDtypeStruct((n_exp, LANES), I32),
        ],
        scratch_shapes=[pltpu.VMEM((n_exp, LANES), F32)],
        compiler_params=_params("arbitrary"),
        name="mix_router",
    )(pool2, attn2, x2, w_pool, w_attn, g, rw_t, rb, lt)


def _split_gate_up_kernel(w_ref, p_ref, g_ref, u_ref):
    for c in range(w_ref.shape[2] // (2 * LANES)):
        w = w_ref[0, :, c * 2 * LANES:(c + 1) * 2 * LANES].astype(BF16)
        y = jnp.dot(w, p_ref[...], preferred_element_type=F32)
        g_ref[0, :, c * LANES:(c + 1) * LANES] = y[:, :LANES].astype(BF16)
        u_ref[0, :, c * LANES:(c + 1) * LANES] = y[:, LANES:].astype(BF16)


def _split_gate_up(w_gate_up, *, td):
    n_exp, d, two_f = w_gate_up.shape
    hidden = two_f // 2
    src = jnp.arange(2 * LANES)[:, None]
    col = jnp.arange(2 * LANES)[None, :]
    pick = (src == jnp.where(col < LANES, 2 * col, 2 * (col - LANES) + 1)).astype(BF16)
    out = jax.ShapeDtypeStruct((n_exp, d, hidden), BF16)
    return pl.pallas_call(
        _split_gate_up_kernel,
        grid=(n_exp, d // td),
        in_specs=[
            pl.BlockSpec((1, td, two_f), lambda e, i: (e, i, 0)),
            pl.BlockSpec((2 * LANES, 2 * LANES), lambda e, i: (0, 0)),
        ],
        out_specs=[pl.BlockSpec((1, td, hidden), lambda e, i: (e, i, 0))] * 2,
        out_shape=[out, out],
        compiler_params=_params("parallel", "parallel"),
        name="split_gate_up",
    )(w_gate_up, pick)


def _moe_kernel(ue_ref, un_ref, tokn_ref, dstp_ref, hn_hbm, wg_ref, wu_ref, wd_ref, bg_ref, bu_ref, bd_ref,
                out_hbm, xf_ref, xb_ref, acc_ref, stage_ref, gsem, ssem, *, chunk, trash_row):
    u = pl.program_id(0)
    f = pl.program_id(1)
    nu = pl.num_programs(0)
    nf = pl.num_programs(1)
    subs = UNIT_ROWS // SUB_ROWS
    nsub = (un_ref[u] + (SUB_ROWS - 1)) // SUB_ROWS

    def whole_gather():
        return pltpu.make_async_copy(hn_hbm.at[pl.ds(0, UNIT_ROWS)], xf_ref, gsem)

    def whole_scatter():
        return pltpu.make_async_copy(stage_ref, out_hbm.at[pl.ds(0, UNIT_ROWS)], ssem)

    @pl.when(jnp.logical_and(u == 0, f == 0))
    def _():
        acc_ref[...] = jnp.zeros_like(acc_ref)
        stage_ref[...] = jnp.zeros_like(stage_ref)

    @pl.when(jnp.logical_and(u > 0, f == 0))
    def _():
        whole_gather().wait()
        for s in range(subs):
            rows = slice(s * SUB_ROWS, (s + 1) * SUB_ROWS)
            xb_ref[rows, :] = xf_ref[rows, :].astype(BF16)

    def issue_chunk(c):
        base = pl.multiple_of(c * chunk, chunk)
        for j in range(chunk):
            r = base + j
            pltpu.make_async_copy(hn_hbm.at[pl.ds(tokn_ref[r], 1)], xf_ref.at[pl.ds(r, 1)], gsem).start()
            pltpu.make_async_copy(stage_ref.at[pl.ds(r, 1)], out_hbm.at[pl.ds(dstp_ref[r], 1)], ssem).start()

    def sub(s, c):
        issue_chunk(f * subs + s)
        r0 = pl.multiple_of(s * SUB_ROWS, SUB_ROWS)
        xs = xb_ref[pl.ds(r0, SUB_ROWS), :]
        gate = jnp.dot(xs, wg_ref[0], preferred_element_type=F32) + bg_ref[0]
        up = jnp.dot(xs, wu_ref[0], preferred_element_type=F32) + bu_ref[0]
        gate = jnp.minimum(gate, SWIGLU_LIMIT)
        up = jnp.clip(up, -SWIGLU_LIMIT, SWIGLU_LIMIT)
        act = gate * jax.nn.sigmoid(SWIGLU_ALPHA * gate) * (up + 1.0)
        y = jnp.dot(act.astype(BF16), wd_ref[0].astype(BF16), preferred_element_type=F32)
        acc_ref[pl.ds(r0, SUB_ROWS), :] = y + jnp.where(f > 0, acc_ref[pl.ds(r0, SUB_ROWS), :], 0.0)
        return c

    lax.fori_loop(0, nsub, sub, 0)

    def rest(s, c):
        issue_chunk(f * subs + s)
        return c

    lax.fori_loop(nsub, subs, rest, 0)

    @pl.when(f == nf - 1)
    def _():
        whole_scatter().wait()
        for s in range(subs):
            rows = slice(s * SUB_ROWS, (s + 1) * SUB_ROWS)
            stage_ref[rows, :] = acc_ref[rows, :] + bd_ref[0]

    @pl.when(jnp.logical_and(u == nu - 1, f == nf - 1))
    def _():
        whole_gather().wait()
        stage_ref[...] = jnp.zeros_like(stage_ref)
        fill = pltpu.make_async_copy(stage_ref, out_hbm.at[pl.ds(trash_row, UNIT_ROWS)], ssem)
        fill.start()
        fill.wait()


def _moe_experts(hn, unit_e, unit_n, row_tok, row_dst, wg, wu, wd, bg, bu, bd, *, n_slots, tf):
    t, d = hn.shape
    n_units = unit_e.shape[0]
    hidden = wg.shape[2]
    nf = hidden // tf
    subs = UNIT_ROWS // SUB_ROWS
    assert UNIT_ROWS % (nf * subs) == 0
    nxt = lambda u, f, ue, un: (jnp.minimum(u + 1, n_units - 1),)
    prv = lambda u, f, ue, un: (jnp.maximum(u - 1, 0),)
    grid_spec = pltpu.PrefetchScalarGridSpec(
        num_scalar_prefetch=2,
        grid=(n_units, nf),
        in_specs=[
            pl.BlockSpec((UNIT_ROWS,), nxt, memory_space=pltpu.SMEM),
            pl.BlockSpec((UNIT_ROWS,), prv, memory_space=pltpu.SMEM),
            pl.BlockSpec(memory_space=pl.ANY),
            pl.BlockSpec((1, d, tf), lambda u, f, ue, un: (ue[u], 0, f)),
            pl.BlockSpec((1, d, tf), lambda u, f, ue, un: (ue[u], 0, f)),
            pl.BlockSpec((1, tf, d), lambda u, f, ue, un: (ue[u], f, 0)),
            pl.BlockSpec((1, 1, tf), lambda u, f, ue, un: (ue[u], 0, f)),
            pl.BlockSpec((1, 1, tf), lambda u, f, ue, un: (ue[u], 0, f)),
            pl.BlockSpec((1, 1, d), lambda u, f, ue, un: (ue[u], 0, 0)),
        ],
        out_specs=pl.BlockSpec(memory_space=pl.ANY),
        scratch_shapes=[
            pltpu.VMEM((UNIT_ROWS, d), F32),
            pltpu.VMEM((UNIT_ROWS, d), BF16),
            pltpu.VMEM((UNIT_ROWS, d), F32),
            pltpu.VMEM((UNIT_ROWS, d), F32),
            pltpu.SemaphoreType.DMA(()),
            pltpu.SemaphoreType.DMA(()),
        ],
    )
    return pl.pallas_call(
        functools.partial(_moe_kernel, chunk=UNIT_ROWS // (nf * subs), trash_row=n_slots - UNIT_ROWS),
        grid_spec=grid_spec,
        out_shape=jax.ShapeDtypeStruct((n_slots, d), F32),
        compiler_params=_params("arbitrary", "arbitrary"),
        name="moe_experts",
    )(unit_e, unit_n, row_tok, row_dst, hn, wg, wu, wd, bg, bu, bd)


def _ple_kernel(h_ref, y0_ref, y1_ref, y2_ref, y3_ref, tg_ref, p_ref, g1_ref, wg_ref, wp_ref, g2_ref, o_ref):
    h = h_ref[...]
    for k, y_ref in enumerate((y0_ref, y1_ref, y2_ref, y3_ref)):
        h = h + y_ref[...] * tg_ref[:, k:k + 1]
    gate = jax.nn.sigmoid(jnp.dot(_rms(h, g1_ref[...]).astype(BF16), wg_ref[...], preferred_element_type=F32))
    e = _rms(jnp.dot(p_ref[...].astype(BF16), wp_ref[...], preferred_element_type=F32), g2_ref[...])
    o_ref[...] = h + gate * e


def _ple_out(h1, slots, gates_t, p2, g1, w_gate, w_proj, g2, *, tm):
    t, d = h1.shape
    pd = p2.shape[1]
    assert TOP_K == 4
    row = lambda i: (i, 0)
    fix = lambda i: (0, 0)
    plane = lambda k: pl.BlockSpec((tm, d), lambda i: (k * (t // tm) + i, 0))
    return pl.pallas_call(
        _ple_kernel,
        grid=(t // tm,),
        in_specs=[
            pl.BlockSpec((tm, d), row),
            plane(0), plane(1), plane(2), plane(3),
            pl.BlockSpec((tm, TOP_K), row),
            pl.BlockSpec((tm, pd), row),
            pl.BlockSpec((1, d), fix),
            pl.BlockSpec((d, d), fix),
            pl.BlockSpec((pd, d), fix),
            pl.BlockSpec((1, d), fix),
        ],
        out_specs=pl.BlockSpec((tm, d), row),
        out_shape=jax.ShapeDtypeStruct((t, d), F32),
        compiler_params=_params("parallel"),
        name="ple_out",
    )(h1, slots, slots, slots, slots, gates_t, p2, g1, w_gate, w_proj, g2)


def _routing_tables(top_e, rank, counts, *, n_units, n_tok):
    n_exp = counts.shape[0]
    experts = jnp.arange(n_exp, dtype=I32)
    units_per = (counts + (UNIT_ROWS - 1)) // UNIT_ROWS
    unit_end = 1 + jnp.cumsum(units_per)
    unit_start = unit_end - units_per
    uid = jnp.arange(n_units, dtype=I32)
    unit_e = jnp.minimum(jnp.sum(unit_end[None, :] <= uid[:, None], axis=1), n_exp - 1).astype(I32)
    mine = unit_e[:, None] == experts[None, :]
    first = jnp.sum(jnp.where(mine, unit_start[None, :], 0), axis=1)
    total = jnp.sum(jnp.where(mine, counts[None, :], 0), axis=1)
    unit_n = jnp.where(uid >= 1, jnp.clip(total - (uid - first) * UNIT_ROWS, 0, UNIT_ROWS), 0).astype(I32)
    start = jnp.sum(jnp.where(top_e[:, :, None] == experts, unit_start, 0), axis=-1)
    row = (start + rank // UNIT_ROWS) * UNIT_ROWS + rank % UNIT_ROWS
    slot = jnp.arange(TOP_K, dtype=I32)[:, None] * n_tok + jnp.arange(n_tok, dtype=I32)[None, :]
    n_rows = n_units * UNIT_ROWS
    row_slot = jnp.full((n_rows,), -1, I32).at[row.reshape(-1)].set(slot.reshape(-1), unique_indices=True)
    valid = row_slot >= 0
    row_tok = jnp.where(valid, row_slot % n_tok, 0).astype(I32)
    trash = TOP_K * n_tok + jnp.arange(n_rows, dtype=I32) % UNIT_ROWS
    row_dst = jnp.where(valid, row_slot, trash).astype(I32)
    return unit_e, unit_n, row_tok, row_dst


def _tile(n, want):
    want = min(n, want)
    assert n % want == 0, (n, want)
    return want


def _layer(h, p_i, ln_mix_g, w_in, pool_w, pool_scale, q_norm_g, k_norm_g, w_out, ln_ffn_g, router_w, router_b,
           w_gate_up, b_gate_up, w_down, b_down, ln_ple_g, ple_gate_w, ple_proj, ple_norm_g):
    b, s, d = h.shape
    t = b * s
    pool_width = d // 2
    attn_width = d // 2
    n_exp = router_w.shape[1]
    hidden = w_down.shape[1]
    assert pool_width % (LANES * len(POOL_WINDOWS)) == 0 and attn_width % LANES == 0
    x2 = h.reshape(t, d)

    proj = _in_proj(x2, ln_mix_g.reshape(1, d), w_in.astype(BF16), tm=_tile(t, 1024), tn=_tile(w_in.shape[1], 1024))
    proj3 = proj.reshape(b, s, -1)
    pool_out = _pool_mixer(proj3, pool_w.astype(BF16), pool_scale.reshape(1, pool_width), width=pool_width)
    attn_out = _stickbreak(proj3, q_norm_g, k_norm_g, pool_width=pool_width, attn_width=attn_width,
                           blk=_tile(s, 256))

    w_out_b = w_out.astype(BF16)
    h1, hn, top_e, gates, rank, counts = _mix_router(
        pool_out.reshape(t, pool_width), attn_out.reshape(t, attn_width), x2,
        w_out_b[:pool_width], w_out_b[pool_width:], ln_ffn_g.reshape(1, d),
        router_w.T.astype(BF16), router_b.reshape(n_exp, 1), tm=_tile(t, 512))

    n_units = t * TOP_K // UNIT_ROWS + n_exp + 2
    unit_e, unit_n, row_tok, row_dst = _routing_tables(top_e, rank, counts[:, 0], n_units=n_units, n_tok=t)
    wg, wu = _split_gate_up(w_gate_up, td=_tile(d, 256))
    bg = b_gate_up[:, 0::2].reshape(n_exp, 1, hidden)
    bu = b_gate_up[:, 1::2].reshape(n_exp, 1, hidden)
    slots = _moe_experts(hn, unit_e, unit_n, row_tok, row_dst, wg, wu, w_down, bg, bu,
                         b_down.reshape(n_exp, 1, d), n_slots=t * TOP_K + UNIT_ROWS, tf=_tile(hidden, 512))

    out = _ple_out(h1, slots, gates.T, p_i.reshape(t, -1),
                   ln_ple_g.reshape(1, d), ple_gate_w.astype(BF16), ple_proj.astype(BF16),
                   ple_norm_g.reshape(1, d), tm=_tile(t, 256))
    return out.reshape(b, s, d)


def kernel(x, p, ln_mix_g, w_in, pool_w, pool_scale, q_norm_g, k_norm_g, w_out, ln_ffn_g, router_w, router_b,
           w_gate_up, b_gate_up, w_down, b_down, ln_ple_g, ple_gate_w, ple_proj, ple_norm_g):
    h = x
    for i in range(p.shape[0]):
        h = _layer(h, p[i], ln_mix_g[i], w_in[i], pool_w[i], pool_scale[i], q_norm_g[i], k_norm_g[i], w_out[i],
                   ln_ffn_g[i], router_w[i], router_b[i], w_gate_up[i], b_gate_up[i], w_down[i], b_down[i],
                   ln_ple_g[i], ple_gate_w[i], ple_proj[i], ple_norm_g[i])
    return h
```

```python
import functools

import jax
import jax.numpy as jnp
from jax import lax
from jax.experimental import pallas as pl
from jax.experimental.pallas import tpu as pltpu

F32 = jnp.float32
BF16 = jnp.bfloat16
I32 = jnp.int32
U32 = jnp.uint32

HEAD_DIM = 64
POOL_WINDOWS = (2, 4, 8, 16)
TOP_K = 4
SWIGLU_LIMIT = 7.0
SWIGLU_ALPHA = 1.702
EPS = 1e-6

LANES = 128
VMEM_LIMIT = 56 * 1024 * 1024
UNIT_ROWS = 1024
SUB_ROWS = 256


def _params(*sem):
    return pltpu.CompilerParams(dimension_semantics=sem, vmem_limit_bytes=VMEM_LIMIT)


def _rms(x, g):
    return x * lax.rsqrt(jnp.mean(x * x, axis=-1, keepdims=True) + EPS) * g


def _pack_halves(x):
    n = x.shape[1] // 2
    lo = lax.bitcast_convert_type(x[:, :n].astype(BF16).astype(F32), U32)
    hi = lax.bitcast_convert_type(x[:, n:].astype(BF16).astype(F32), U32)
    return hi | (lo >> 16)


def _unpack_halves(w):
    lo = lax.bitcast_convert_type(w << 16, F32)
    hi = lax.bitcast_convert_type(w & jnp.uint32(0xFFFF0000), F32)
    return lo, hi


def _split_bf16(x):
    hi = x.astype(BF16)
    lo = (x - hi.astype(F32)).astype(BF16)
    return hi, lo


def _in_proj_kernel(x_ref, g_ref, w_ref, o_ref, xn_ref):
    @pl.when(pl.program_id(1) == 0)
    def _():
        xn_ref[...] = _rms(x_ref[...], g_ref[...]).astype(BF16)

    o_ref[...] = jnp.dot(xn_ref[...], w_ref[...], preferred_element_type=F32)


def _in_proj(x2, g, w, *, tm, tn):
    t, d = x2.shape
    n = w.shape[1]
    return pl.pallas_call(
        _in_proj_kernel,
        grid=(t // tm, n // tn),
        in_specs=[
            pl.BlockSpec((tm, d), lambda i, j: (i, 0)),
            pl.BlockSpec((1, d), lambda i, j: (0, 0)),
            pl.BlockSpec((d, tn), lambda i, j: (0, j)),
        ],
        out_specs=pl.BlockSpec((tm, tn), lambda i, j: (i, j)),
        out_shape=jax.ShapeDtypeStruct((t, n), F32),
        scratch_shapes=[pltpu.VMEM((tm, d), BF16)],
        compiler_params=_params("parallel", "arbitrary"),
        name="in_proj",
    )(x2, g, w)


def _pool_kernel(u_ref, w_ref, s_ref, o_ref, *, group):
    seq = u_ref.shape[1]
    row = lax.broadcasted_iota(I32, (seq, 1), 0)
    for gi, win in enumerate(POOL_WINDOWS):
        sl = slice(gi * group, (gi + 1) * group)
        u = u_ref[0, :, sl]
        acc = u
        span = 1
        while span < win:
            shifted = pltpu.roll(acc, span, axis=0)
            acc = acc + jnp.where(row >= span, shifted, 0.0)
            span *= 2
        count = jnp.minimum(row + 1, win).astype(F32)
        d = (acc / count - u).astype(BF16)
        y = jnp.dot(d, w_ref[gi], preferred_element_type=F32)
        o_ref[0, :, sl] = (y * s_ref[:, sl]).astype(BF16)


def _pool_mixer(proj3, pool_w, pool_scale, *, width):
    b, s, _ = proj3.shape
    group = width // len(POOL_WINDOWS)
    return pl.pallas_call(
        functools.partial(_pool_kernel, group=group),
        grid=(b,),
        in_specs=[
            pl.BlockSpec((1, s, width), lambda i: (i, 0, 0)),
            pl.BlockSpec((len(POOL_WINDOWS), group, group), lambda i: (0, 0, 0)),
            pl.BlockSpec((1, width), lambda i: (0, 0)),
        ],
        out_specs=pl.BlockSpec((1, s, width), lambda i: (i, 0, 0)),
        out_shape=jax.ShapeDtypeStruct((b, s, width), BF16),
        compiler_params=_params("parallel"),
        name="pool_mixer",
    )(proj3, pool_w, pool_scale)


def _head_rmsnorm(x, g2, bd):
    hi, lo = _split_bf16(x * x)
    ss = jnp.dot(hi, bd, preferred_element_type=F32) + jnp.dot(lo, bd, preferred_element_type=F32)
    return x * lax.rsqrt(ss * (1.0 / HEAD_DIM) + EPS) * g2


def _stickbreak_kernel(q_ref, k_ref, v_ref, qg_ref, kg_ref, bd_ref, tri_ref, o_ref,
                       kn_ref, vb_ref, q2_ref, carry_ref, acc_ref, *, blk):
    qi = pl.program_id(2)
    lane = lax.broadcasted_iota(I32, (1, LANES), 1)
    head0 = lane < HEAD_DIM

    @pl.when(qi == 0)
    def _():
        kn_ref[...] = _head_rmsnorm(k_ref[0], kg_ref[...], bd_ref[...]).astype(BF16)
        vb_ref[...] = v_ref[0].astype(BF16)

    q = _head_rmsnorm(q_ref[0], qg_ref[...], bd_ref[...]) * (HEAD_DIM ** -0.5)
    q2_ref[...] = jnp.concatenate([jnp.where(head0, q, 0.0), jnp.where(head0, 0.0, q)], axis=0).astype(BF16)
    carry_ref[...] = jnp.zeros_like(carry_ref)
    acc_ref[...] = jnp.zeros_like(acc_ref)

    heads = (slice(0, blk), slice(blk, 2 * blk))

    def tiles(blocks, masked):
        starts = [j * blk if isinstance(j, int) else pl.multiple_of(j * blk, blk) for j in blocks]
        chains = [(ks, rows) for ks in starts for rows in heads]
        zs = [lax.dot_general(q2_ref[rows, :], kn_ref[pl.ds(ks, blk), :], (((1,), (1,)), ((), ())),
                              preferred_element_type=F32) for ks, rows in chains]
        csums = []
        for i, z in enumerate(zs):
            sp = jnp.maximum(z, 0.0) + jnp.log(1.0 + jnp.exp(-jnp.abs(z)))
            if masked:
                r = lax.broadcasted_iota(I32, (blk, blk), 0)
                c = lax.broadcasted_iota(I32, (blk, blk), 1)
                keep = c < r
                sp = jnp.where(keep, sp, 0.0)
                zs[i] = jnp.where(keep, z, -1e30)
            csums.append(jnp.dot(sp.astype(BF16), tri_ref[:blk, :], preferred_element_type=F32))
        for (ks, rows), z, csum in zip(chains, zs, csums):
            carry = carry_ref[rows, :]
            a = jnp.concatenate(
                [jnp.exp(z[:, h * LANES:(h + 1) * LANES] - csum[:, h * LANES:(h + 1) * LANES] - carry)
                 for h in range(blk // LANES)], axis=1)
            acc_ref[rows, :] += jnp.dot(a.astype(BF16), vb_ref[pl.ds(ks, blk), :], preferred_element_type=F32)
            carry_ref[rows, :] = carry + jnp.broadcast_to(csum[:, 0:1], carry.shape)

    tiles([qi], True)

    def body(step, c):
        j = qi - 1 - 2 * step
        tiles([j, j - 1], False)
        return c

    lax.fori_loop(0, qi // 2, body, 0)

    @pl.when(qi % 2 == 1)
    def _():
        tiles([0], False)

    o_ref[0] = jnp.where(head0, acc_ref[:blk], acc_ref[blk:]).astype(BF16)


def _stickbreak(proj3, q_norm_g, k_norm_g, *, pool_width, attn_width, blk):
    b, s, _ = proj3.shape
    pairs = attn_width // LANES
    qoff = pool_width // LANES
    koff = qoff + pairs
    voff = koff + pairs
    qg2 = jnp.concatenate([q_norm_g, q_norm_g]).reshape(1, LANES).astype(F32)
    kg2 = jnp.concatenate([k_norm_g, k_norm_g]).reshape(1, LANES).astype(F32)
    li = jnp.arange(LANES) // HEAD_DIM
    bd = (li[:, None] == li[None, :]).astype(BF16)
    tr = (jnp.arange(blk)[:, None] >= jnp.arange(blk)[None, :]).astype(BF16)
    tri = jnp.concatenate([tr, tr], axis=0)
    return pl.pallas_call(
        functools.partial(_stickbreak_kernel, blk=blk),
        grid=(b, pairs, s // blk),
        in_specs=[
            pl.BlockSpec((1, blk, LANES), lambda i, h, j: (i, j, qoff + h)),
            pl.BlockSpec((1, s, LANES), lambda i, h, j: (i, 0, koff + h)),
            pl.BlockSpec((1, s, LANES), lambda i, h, j: (i, 0, voff + h)),
            pl.BlockSpec((1, LANES), lambda i, h, j: (0, 0)),
            pl.BlockSpec((1, LANES), lambda i, h, j: (0, 0)),
            pl.BlockSpec((LANES, LANES), lambda i, h, j: (0, 0)),
            pl.BlockSpec((2 * blk, blk), lambda i, h, j: (0, 0)),
        ],
        out_specs=pl.BlockSpec((1, blk, LANES), lambda i, h, j: (i, j, h)),
        out_shape=jax.ShapeDtypeStruct((b, s, attn_width), BF16),
        scratch_shapes=[
            pltpu.VMEM((s, LANES), BF16),
            pltpu.VMEM((s, LANES), BF16),
            pltpu.VMEM((2 * blk, LANES), BF16),
            pltpu.VMEM((2 * blk, LANES), F32),
            pltpu.VMEM((2 * blk, LANES), F32),
        ],
        compiler_params=_params("parallel", "parallel", "arbitrary"),
        name="stickbreak",
    )(proj3, proj3, proj3, qg2, kg2, bd, tri)


def _mix_router_kernel(pool_ref, attn_ref, x_ref, wp_ref, wa_ref, g_ref, rw_ref, rb_ref, lt_ref,
                       h_ref, hn_ref, te_ref, tg_ref, rk_ref, cnt_ref, run_ref, *, n_exp):
    i = pl.program_id(0)

    @pl.when(i == 0)
    def _():
        run_ref[...] = jnp.zeros_like(run_ref)

    h = (x_ref[...]
         + jnp.dot(pool_ref[...], wp_ref[...], preferred_element_type=F32)
         + jnp.dot(attn_ref[...], wa_ref[...], preferred_element_type=F32))
    h_ref[...] = h
    hn = _rms(h, g_ref[...])
    hn_ref[...] = _pack_halves(hn)
    logits = lax.dot_general(rw_ref[...], hn.astype(BF16), (((1,), (1,)), ((), ())),
                             preferred_element_type=F32) + rb_ref[...]
    tm = logits.shape[1]
    eidx = lax.broadcasted_iota(I32, (n_exp, tm), 0).astype(F32)
    vals, idxs, hots = [], [], []
    rem = logits
    for _ in range(TOP_K):
        m = jnp.max(rem, axis=0, keepdims=True)
        first = jnp.min(jnp.where(rem == m, eidx, float(n_exp)), axis=0, keepdims=True)
        hot = eidx == first
        vals.append(m)
        idxs.append(first)
        hots.append(hot)
        rem = jnp.where(hot, -jnp.inf, rem)
    ex = [jnp.exp(v - vals[0]) for v in vals]
    den = ex[0] + ex[1] + ex[2] + ex[3]
    tg_ref[...] = jnp.concatenate([e / den for e in ex], axis=0)
    te_ref[...] = jnp.concatenate(idxs, axis=0).astype(I32)
    hot_all = (hots[0] | hots[1] | hots[2] | hots[3]).astype(F32)
    before = jnp.dot(hot_all.astype(BF16), lt_ref[...], preferred_element_type=F32)
    base = run_ref[:, 0:1] + before
    rk_ref[...] = jnp.concatenate(
        [jnp.sum(jnp.where(hot, base, 0.0), axis=0, keepdims=True) for hot in hots], axis=0).astype(I32)
    run_ref[...] = run_ref[...] + jnp.sum(hot_all, axis=1, keepdims=True)
    cnt_ref[...] = run_ref[...].astype(I32)


def _mix_router(pool2, attn2, x2, w_pool, w_attn, g, rw_t, rb, *, tm):
    t, d = x2.shape
    half = pool2.shape[1]
    n_exp = rw_t.shape[0]
    lt = (jnp.arange(tm)[:, None] < jnp.arange(tm)[None, :]).astype(BF16)
    row = lambda i: (i, 0)
    fix = lambda i: (0, 0)
    col = lambda i: (0, i)
    return pl.pallas_call(
        functools.partial(_mix_router_kernel, n_exp=n_exp),
        grid=(t // tm,),
        in_specs=[
            pl.BlockSpec((tm, half), row),
            pl.BlockSpec((tm, half), row),
            pl.BlockSpec((tm, d), row),
            pl.BlockSpec((half, d), fix),
            pl.BlockSpec((half, d), fix),
            pl.BlockSpec((1, d), fix),
            pl.BlockSpec((n_exp, d), fix),
            pl.BlockSpec((n_exp, 1), fix),
            pl.BlockSpec((tm, tm), fix),
        ],
        out_specs=[
            pl.BlockSpec((tm, d), row),
            pl.BlockSpec((tm, d // 2), row),
            pl.BlockSpec((TOP_K, tm), col),
            pl.BlockSpec((TOP_K, tm), col),
            pl.BlockSpec((TOP_K, tm), col),
            pl.BlockSpec((n_exp, LANES), fix),
        ],
        out_shape=[
            jax.ShapeDtypeStruct((t, d), F32),
            jax.ShapeDtypeStruct((t, d // 2), U32),
            jax.ShapeDtypeStruct((TOP_K, t), I32),
            jax.ShapeDtypeStruct((TOP_K, t), F32),
            jax.ShapeDtypeStruct((TOP_K, t), I32),
            jax.ShapeDtypeStruct((n_exp, LANES), I32),
        ],
        scratch_shapes=[pltpu.VMEM((n_exp, LANES), F32)],
        compiler_params=_params("arbitrary"),
        name="mix_router",
    )(pool2, attn2, x2, w_pool, w_attn, g, rw_t, rb, lt)


def _split_gate_up_kernel(w_ref, p_ref, g_ref, u_ref):
    for c in range(w_ref.shape[2] // (2 * LANES)):
        w = w_ref[0, :, c * 2 * LANES:(c + 1) * 2 * LANES].astype(BF16)
        y = jnp.dot(w, p_ref[...], preferred_element_type=F32)
        g_ref[0, :, c * LANES:(c + 1) * LANES] = y[:, :LANES].astype(BF16)
        u_ref[0, :, c * LANES:(c + 1) * LANES] = y[:, LANES:].astype(BF16)


def _split_gate_up(w_gate_up, *, td):
    n_exp, d, two_f = w_gate_up.shape
    hidden = two_f // 2
    src = jnp.arange(2 * LANES)[:, None]
    col = jnp.arange(2 * LANES)[None, :]
    pick = (src == jnp.where(col < LANES, 2 * col, 2 * (col - LANES) + 1)).astype(BF16)
    out = jax.ShapeDtypeStruct((n_exp, d, hidden), BF16)
    return pl.pallas_call(
        _split_gate_up_kernel,
        grid=(n_exp, d // td),
        in_specs=[
            pl.BlockSpec((1, td, two_f), lambda e, i: (e, i, 0)),
            pl.BlockSpec((2 * LANES, 2 * LANES), lambda e, i: (0, 0)),
        ],
        out_specs=[pl.BlockSpec((1, td, hidden), lambda e, i: (e, i, 0))] * 2,
        out_shape=[out, out],
        compiler_params=_params("parallel", "parallel"),
        name="split_gate_up",
    )(w_gate_up, pick)


def _moe_kernel(ue_ref, un_ref, tokn_ref, dstp_ref, hn_hbm, wg_ref, wu_ref, wd_ref, bg_ref, bu_ref, bd_ref,
                out_hbm, xf_ref, xb_ref, acc_ref, stage_ref, gsem, ssem, *, chunk, trash_row):
    u = pl.program_id(0)
    f = pl.program_id(1)
    nu = pl.num_programs(0)
    nf = pl.num_programs(1)
    subs = UNIT_ROWS // SUB_ROWS
    half = xb_ref.shape[1] // 2
    nsub = (un_ref[u] + (SUB_ROWS - 1)) // SUB_ROWS
    live = un_ref[u] > 0
    prev_live = jnp.logical_and(u > 0, un_ref[jnp.maximum(u - 1, 0)] > 0)
    next_live = jnp.logical_and(u + 1 < nu, un_ref[jnp.minimum(u + 1, nu - 1)] > 0)
    gathered = jnp.logical_or(live, prev_live)
    gathers_next = jnp.logical_or(live, next_live)
    scatters_prev = jnp.logical_or(live, prev_live)

    def whole_gather():
        return pltpu.make_async_copy(hn_hbm.at[pl.ds(0, UNIT_ROWS)], xf_ref, gsem)

    def whole_scatter():
        return pltpu.make_async_copy(stage_ref, out_hbm.at[pl.ds(0, UNIT_ROWS)], ssem)

    @pl.when(jnp.logical_and(u == 0, f == 0))
    def _():
        acc_ref[...] = jnp.zeros_like(acc_ref)
        stage_ref[...] = jnp.zeros_like(stage_ref)

    @pl.when(jnp.logical_and(jnp.logical_and(u > 0, f == 0), gathered))
    def _():
        whole_gather().wait()
        for s in range(subs):
            rows = slice(s * SUB_ROWS, (s + 1) * SUB_ROWS)
            lo, hi = _unpack_halves(xf_ref[rows, :])
            xb_ref[rows, :half] = lo.astype(BF16)
            xb_ref[rows, half:] = hi.astype(BF16)

    def gather_chunk(c):
        base = pl.multiple_of(c * chunk, chunk)
        for j in range(chunk):
            r = base + j
            pltpu.make_async_copy(hn_hbm.at[pl.ds(tokn_ref[r], 1)], xf_ref.at[pl.ds(r, 1)], gsem).start()

    def scatter_chunk(c):
        base = pl.multiple_of(c * chunk, chunk)
        for j in range(chunk):
            r = base + j
            pltpu.make_async_copy(stage_ref.at[pl.ds(r, 1)], out_hbm.at[pl.ds(dstp_ref[r], 1)], ssem).start()

    def sub(s, c):
        gather_chunk(f * subs + s)
        scatter_chunk(f * subs + s)
        r0 = pl.multiple_of(s * SUB_ROWS, SUB_ROWS)
        xs = xb_ref[pl.ds(r0, SUB_ROWS), :]
        gate = jnp.dot(xs, wg_ref[0], preferred_element_type=F32) + bg_ref[0]
        up = jnp.dot(xs, wu_ref[0], preferred_element_type=F32) + bu_ref[0]
        gate = jnp.minimum(gate, SWIGLU_LIMIT)
        up = jnp.clip(up, -SWIGLU_LIMIT, SWIGLU_LIMIT)
        act = gate * jax.nn.sigmoid(SWIGLU_ALPHA * gate) * (up + 1.0)
        y = jnp.dot(act.astype(BF16), wd_ref[0].astype(BF16), preferred_element_type=F32)
        acc_ref[pl.ds(r0, SUB_ROWS), :] = y + jnp.where(f > 0, acc_ref[pl.ds(r0, SUB_ROWS), :], 0.0)
        return c

    lax.fori_loop(0, nsub, sub, 0)

    def rest(s, c):
        @pl.when(gathers_next)
        def _():
            gather_chunk(f * subs + s)

        @pl.when(scatters_prev)
        def _():
            scatter_chunk(f * subs + s)

        return c

    lax.fori_loop(nsub, subs, rest, 0)

    @pl.when(jnp.logical_and(f == nf - 1, scatters_prev))
    def _():
        whole_scatter().wait()

    @pl.when(jnp.logical_and(f == nf - 1, live))
    def _():
        for s in range(subs):
            rows = slice(s * SUB_ROWS, (s + 1) * SUB_ROWS)
            stage_ref[rows, :] = _pack_halves(acc_ref[rows, :] + bd_ref[0])

    @pl.when(jnp.logical_and(u == nu - 1, f == nf - 1))
    def _():
        stage_ref[...] = jnp.zeros_like(stage_ref)
        fill = pltpu.make_async_copy(stage_ref, out_hbm.at[pl.ds(trash_row, UNIT_ROWS)], ssem)
        fill.start()
        fill.wait()


def _moe_experts(hn, unit_e, unit_n, row_tok, row_dst, wg, wu, wd, bg, bu, bd, *, n_slots, tf):
    d = wd.shape[2]
    assert hn.shape[1] * 2 == d and hn.dtype == U32
    n_units = unit_e.shape[0]
    hidden = wg.shape[2]
    nf = hidden // tf
    subs = UNIT_ROWS // SUB_ROWS
    assert UNIT_ROWS % (nf * subs) == 0
    nxt = lambda u, f, ue, un: (jnp.minimum(u + 1, n_units - 1),)
    prv = lambda u, f, ue, un: (jnp.maximum(u - 1, 0),)
    grid_spec = pltpu.PrefetchScalarGridSpec(
        num_scalar_prefetch=2,
        grid=(n_units, nf),
        in_specs=[
            pl.BlockSpec((UNIT_ROWS,), nxt, memory_space=pltpu.SMEM),
            pl.BlockSpec((UNIT_ROWS,), prv, memory_space=pltpu.SMEM),
            pl.BlockSpec(memory_space=pl.ANY),
            pl.BlockSpec((1, d, tf), lambda u, f, ue, un: (ue[u], 0, f)),
            pl.BlockSpec((1, d, tf), lambda u, f, ue, un: (ue[u], 0, f)),
            pl.BlockSpec((1, tf, d), lambda u, f, ue, un: (ue[u], f, 0)),
            pl.BlockSpec((1, 1, tf), lambda u, f, ue, un: (ue[u], 0, f)),
            pl.BlockSpec((1, 1, tf), lambda u, f, ue, un: (ue[u], 0, f)),
            pl.BlockSpec((1, 1, d), lambda u, f, ue, un: (ue[u], 0, 0)),
        ],
        out_specs=pl.BlockSpec(memory_space=pl.ANY),
        scratch_shapes=[
            pltpu.VMEM((UNIT_ROWS, d // 2), U32),
            pltpu.VMEM((UNIT_ROWS, d), BF16),
            pltpu.VMEM((UNIT_ROWS, d), F32),
            pltpu.VMEM((UNIT_ROWS, d // 2), U32),
            pltpu.SemaphoreType.DMA(()),
            pltpu.SemaphoreType.DMA(()),
        ],
    )
    return pl.pallas_call(
        functools.partial(_moe_kernel, chunk=UNIT_ROWS // (nf * subs), trash_row=n_slots - UNIT_ROWS),
        grid_spec=grid_spec,
        out_shape=jax.ShapeDtypeStruct((n_slots, d // 2), U32),
        compiler_params=_params("arbitrary", "arbitrary"),
        name="moe_experts",
    )(unit_e, unit_n, row_tok, row_dst, hn, wg, wu, wd, bg, bu, bd)


def _ple_kernel(h_ref, y0_ref, y1_ref, y2_ref, y3_ref, tg_ref, p_ref, g1_ref, wg_ref, wp_ref, g2_ref, o_ref):
    d = h_ref.shape[1]
    lo = h_ref[:, :d // 2]
    hi = h_ref[:, d // 2:]
    for k, y_ref in enumerate((y0_ref, y1_ref, y2_ref, y3_ref)):
        y_lo, y_hi = _unpack_halves(y_ref[...])
        lo = lo + y_lo * tg_ref[:, k:k + 1]
        hi = hi + y_hi * tg_ref[:, k:k + 1]
    h = jnp.concatenate([lo, hi], axis=1)
    gate = jax.nn.sigmoid(jnp.dot(_rms(h, g1_ref[...]).astype(BF16), wg_ref[...], preferred_element_type=F32))
    e = _rms(jnp.dot(p_ref[...].astype(BF16), wp_ref[...], preferred_element_type=F32), g2_ref[...])
    o_ref[...] = h + gate * e


def _ple_out(h1, slots, gates_t, p2, g1, w_gate, w_proj, g2, *, tm):
    t, d = h1.shape
    pd = p2.shape[1]
    assert TOP_K == 4
    row = lambda i: (i, 0)
    fix = lambda i: (0, 0)
    plane = lambda k: pl.BlockSpec((tm, d // 2), lambda i: (k * (t // tm) + i, 0))
    return pl.pallas_call(
        _ple_kernel,
        grid=(t // tm,),
        in_specs=[
            pl.BlockSpec((tm, d), row),
            plane(0), plane(1), plane(2), plane(3),
            pl.BlockSpec((tm, TOP_K), row),
            pl.BlockSpec((tm, pd), row),
            pl.BlockSpec((1, d), fix),
            pl.BlockSpec((d, d), fix),
            pl.BlockSpec((pd, d), fix),
            pl.BlockSpec((1, d), fix),
        ],
        out_specs=pl.BlockSpec((tm, d), row),
        out_shape=jax.ShapeDtypeStruct((t, d), F32),
        compiler_params=_params("parallel"),
        name="ple_out",
    )(h1, slots, slots, slots, slots, gates_t, p2, g1, w_gate, w_proj, g2)


def _routing_tables(top_e, rank, counts, *, n_units, n_tok):
    n_exp = counts.shape[0]
    experts = jnp.arange(n_exp, dtype=I32)
    units_per = (counts + (UNIT_ROWS - 1)) // UNIT_ROWS
    unit_end = 1 + jnp.cumsum(units_per)
    unit_start = unit_end - units_per
    uid = jnp.arange(n_units, dtype=I32)
    unit_e = jnp.minimum(jnp.sum(unit_end[None, :] <= uid[:, None], axis=1), n_exp - 1).astype(I32)
    mine = unit_e[:, None] == experts[None, :]
    first = jnp.sum(jnp.where(mine, unit_start[None, :], 0), axis=1)
    total = jnp.sum(jnp.where(mine, counts[None, :], 0), axis=1)
    unit_n = jnp.where(uid >= 1, jnp.clip(total - (uid - first) * UNIT_ROWS, 0, UNIT_ROWS), 0).astype(I32)
    start = jnp.sum(jnp.where(top_e[:, :, None] == experts, unit_start, 0), axis=-1)
    row = (start + rank // UNIT_ROWS) * UNIT_ROWS + rank % UNIT_ROWS
    slot = jnp.arange(TOP_K, dtype=I32)[:, None] * n_tok + jnp.arange(n_tok, dtype=I32)[None, :]
    n_rows = n_units * UNIT_ROWS
    row_slot = jnp.full((n_rows,), -1, I32).at[row.reshape(-1)].set(slot.reshape(-1), unique_indices=True)
    valid = row_slot >= 0
    row_tok = jnp.where(valid, row_slot % n_tok, 0).astype(I32)
    trash = TOP_K * n_tok + jnp.arange(n_rows, dtype=I32) % UNIT_ROWS
    row_dst = jnp.where(valid, row_slot, trash).astype(I32)
    return unit_e, unit_n, row_tok, row_dst


def _tile(n, want):
    want = min(n, want)
    assert n % want == 0, (n, want)
    return want


def _layer(h, p_i, ln_mix_g, w_in, pool_w, pool_scale, q_norm_g, k_norm_g, w_out, ln_ffn_g, router_w, router_b,
           w_gate_up, b_gate_up, w_down, b_down, ln_ple_g, ple_gate_w, ple_proj, ple_norm_g):
    b, s, d = h.shape
    t = b * s
    pool_width = d // 2
    attn_width = d // 2
    n_exp = router_w.shape[1]
    hidden = w_down.shape[1]
    assert pool_width % (LANES * len(POOL_WINDOWS)) == 0 and attn_width % LANES == 0
    x2 = h.reshape(t, d)

    proj = _in_proj(x2, ln_mix_g.reshape(1, d), w_in.astype(BF16), tm=_tile(t, 1024), tn=_tile(w_in.shape[1], 1024))
    proj3 = proj.reshape(b, s, -1)
    pool_out = _pool_mixer(proj3, pool_w.astype(BF16), pool_scale.reshape(1, pool_width), width=pool_width)
    attn_out = _stickbreak(proj3, q_norm_g, k_norm_g, pool_width=pool_width, attn_width=attn_width,
                           blk=_tile(s, 256))

    w_out_b = w_out.astype(BF16)
    h1, hn, top_e, gates, rank, counts = _mix_router(
        pool_out.reshape(t, pool_width), attn_out.reshape(t, attn_width), x2,
        w_out_b[:pool_width], w_out_b[pool_width:], ln_ffn_g.reshape(1, d),
        router_w.T.astype(BF16), router_b.reshape(n_exp, 1), tm=_tile(t, 512))

    n_units = t * TOP_K // UNIT_ROWS + n_exp + 2
    unit_e, unit_n, row_tok, row_dst = _routing_tables(top_e, rank, counts[:, 0], n_units=n_units, n_tok=t)
    wg, wu = _split_gate_up(w_gate_up, td=_tile(d, 256))
    bg = b_gate_up[:, 0::2].reshape(n_exp, 1, hidden)
    bu = b_gate_up[:, 1::2].reshape(n_exp, 1, hidden)
    slots = _moe_experts(hn, unit_e, unit_n, row_tok, row_dst, wg, wu, w_down, bg, bu,
                         b_down.reshape(n_exp, 1, d), n_slots=t * TOP_K + UNIT_ROWS, tf=_tile(hidden, 512))

    out = _ple_out(h1, slots, gates.T, p_i.reshape(t, -1),
                   ln_ple_g.reshape(1, d), ple_gate_w.astype(BF16), ple_proj.astype(BF16),
                   ple_norm_g.reshape(1, d), tm=_tile(t, 256))
    return out.reshape(b, s, d)


def kernel(x, p, ln_mix_g, w_in, pool_w, pool_scale, q_norm_g, k_norm_g, w_out, ln_ffn_g, router_w, router_b,
           w_gate_up, b_gate_up, w_down, b_down, ln_ple_g, ple_gate_w, ple_proj, ple_norm_g):
    h = x
    for i in range(p.shape[0]):
        h = _layer(h, p[i], ln_mix_g[i], w_in[i], pool_w[i], pool_scale[i], q_norm_g[i], k_norm_g[i], w_out[i],
                   ln_ffn_g[i], router_w[i], router_b[i], w_gate_up[i], b_gate_up[i], w_down[i], b_down[i],
                   ln_ple_g[i], ple_gate_w[i], ple_proj[i], ple_norm_g[i])
    return h
```

```python
import functools

import jax
import jax.numpy as jnp
from jax import lax
from jax.experimental import pallas as pl
from jax.experimental.pallas import tpu as pltpu

F32 = jnp.float32
BF16 = jnp.bfloat16
I32 = jnp.int32
U32 = jnp.uint32

HEAD_DIM = 64
POOL_WINDOWS = (2, 4, 8, 16)
TOP_K = 4
SWIGLU_LIMIT = 7.0
SWIGLU_ALPHA = 1.702
EPS = 1e-6

LANES = 128
VMEM_LIMIT = 56 * 1024 * 1024
UNIT_ROWS = 1024
SUB_ROWS = 256


def _params(*sem):
    return pltpu.CompilerParams(dimension_semantics=sem, vmem_limit_bytes=VMEM_LIMIT)


def _rms(x, g):
    return x * lax.rsqrt(jnp.mean(x * x, axis=-1, keepdims=True) + EPS) * g


def _pack_halves(x):
    n = x.shape[1] // 2
    lo = lax.bitcast_convert_type(x[:, :n].astype(BF16).astype(F32), U32)
    hi = lax.bitcast_convert_type(x[:, n:].astype(BF16).astype(F32), U32)
    return hi | (lo >> 16)


def _unpack_halves(w):
    lo = lax.bitcast_convert_type(w << 16, F32)
    hi = lax.bitcast_convert_type(w & jnp.uint32(0xFFFF0000), F32)
    return lo, hi


def _store_row_tiles(ref, first, packed):
    n, width = packed.shape
    m = width // LANES
    for c in range(m):
        ref[pl.ds(first * m + c, n, stride=m), :] = packed[:, c * LANES:(c + 1) * LANES]


def _load_row_tiles(ref, first, n, m):
    return jnp.concatenate([ref[pl.ds(first * m + c, n, stride=m), :] for c in range(m)], axis=1)


def _split_bf16(x):
    hi = x.astype(BF16)
    lo = (x - hi.astype(F32)).astype(BF16)
    return hi, lo


def _in_proj_kernel(x_ref, g_ref, w_ref, o_ref, xn_ref):
    @pl.when(pl.program_id(1) == 0)
    def _():
        xn_ref[...] = _rms(x_ref[...], g_ref[...]).astype(BF16)

    o_ref[...] = jnp.dot(xn_ref[...], w_ref[...], preferred_element_type=F32)


def _in_proj(x2, g, w, *, tm, tn):
    t, d = x2.shape
    n = w.shape[1]
    return pl.pallas_call(
        _in_proj_kernel,
        grid=(t // tm, n // tn),
        in_specs=[
            pl.BlockSpec((tm, d), lambda i, j: (i, 0)),
            pl.BlockSpec((1, d), lambda i, j: (0, 0)),
            pl.BlockSpec((d, tn), lambda i, j: (0, j)),
        ],
        out_specs=pl.BlockSpec((tm, tn), lambda i, j: (i, j)),
        out_shape=jax.ShapeDtypeStruct((t, n), F32),
        scratch_shapes=[pltpu.VMEM((tm, d), BF16)],
        compiler_params=_params("parallel", "arbitrary"),
        name="in_proj",
    )(x2, g, w)


def _pool_kernel(u_ref, w_ref, s_ref, o_ref, *, group):
    seq = u_ref.shape[1]
    row = lax.broadcasted_iota(I32, (seq, 1), 0)
    for gi, win in enumerate(POOL_WINDOWS):
        sl = slice(gi * group, (gi + 1) * group)
        u = u_ref[0, :, sl]
        acc = u
        span = 1
        while span < win:
            shifted = pltpu.roll(acc, span, axis=0)
            acc = acc + jnp.where(row >= span, shifted, 0.0)
            span *= 2
        count = jnp.minimum(row + 1, win).astype(F32)
        d = (acc / count - u).astype(BF16)
        y = jnp.dot(d, w_ref[gi], preferred_element_type=F32)
        o_ref[0, :, sl] = (y * s_ref[:, sl]).astype(BF16)


def _pool_mixer(proj3, pool_w, pool_scale, *, width):
    b, s, _ = proj3.shape
    group = width // len(POOL_WINDOWS)
    return pl.pallas_call(
        functools.partial(_pool_kernel, group=group),
        grid=(b,),
        in_specs=[
            pl.BlockSpec((1, s, width), lambda i: (i, 0, 0)),
            pl.BlockSpec((len(POOL_WINDOWS), group, group), lambda i: (0, 0, 0)),
            pl.BlockSpec((1, width), lambda i: (0, 0)),
        ],
        out_specs=pl.BlockSpec((1, s, width), lambda i: (i, 0, 0)),
        out_shape=jax.ShapeDtypeStruct((b, s, width), BF16),
        compiler_params=_params("parallel"),
        name="pool_mixer",
    )(proj3, pool_w, pool_scale)


def _head_rmsnorm(x, g2, bd):
    hi, lo = _split_bf16(x * x)
    ss = jnp.dot(hi, bd, preferred_element_type=F32) + jnp.dot(lo, bd, preferred_element_type=F32)
    return x * lax.rsqrt(ss * (1.0 / HEAD_DIM) + EPS) * g2


def _stickbreak_kernel(q_ref, k_ref, v_ref, qg_ref, kg_ref, bd_ref, tri_ref, o_ref,
                       kn_ref, vb_ref, q2_ref, carry_ref, acc_ref, *, blk):
    qi = pl.program_id(2)
    lane = lax.broadcasted_iota(I32, (1, LANES), 1)
    head0 = lane < HEAD_DIM

    @pl.when(qi == 0)
    def _():
        kn_ref[...] = _head_rmsnorm(k_ref[0], kg_ref[...], bd_ref[...]).astype(BF16)
        vb_ref[...] = v_ref[0].astype(BF16)

    q = _head_rmsnorm(q_ref[0], qg_ref[...], bd_ref[...]) * (HEAD_DIM ** -0.5)
    q2_ref[...] = jnp.concatenate([jnp.where(head0, q, 0.0), jnp.where(head0, 0.0, q)], axis=0).astype(BF16)
    carry_ref[...] = jnp.zeros_like(carry_ref)
    acc_ref[...] = jnp.zeros_like(acc_ref)

    heads = (slice(0, blk), slice(blk, 2 * blk))

    def tiles(blocks, masked):
        starts = [j * blk if isinstance(j, int) else pl.multiple_of(j * blk, blk) for j in blocks]
        chains = [(ks, rows) for ks in starts for rows in heads]
        zs = [lax.dot_general(q2_ref[rows, :], kn_ref[pl.ds(ks, blk), :], (((1,), (1,)), ((), ())),
                              preferred_element_type=F32) for ks, rows in chains]
        csums = []
        for i, z in enumerate(zs):
            sp = jnp.maximum(z, 0.0) + jnp.log(1.0 + jnp.exp(-jnp.abs(z)))
            if masked:
                r = lax.broadcasted_iota(I32, (blk, blk), 0)
                c = lax.broadcasted_iota(I32, (blk, blk), 1)
                keep = c < r
                sp = jnp.where(keep, sp, 0.0)
                zs[i] = jnp.where(keep, z, -1e30)
            csums.append(jnp.dot(sp.astype(BF16), tri_ref[:blk, :], preferred_element_type=F32))
        for (ks, rows), z, csum in zip(chains, zs, csums):
            carry = carry_ref[rows, :]
            a = jnp.concatenate(
                [jnp.exp(z[:, h * LANES:(h + 1) * LANES] - csum[:, h * LANES:(h + 1) * LANES] - carry)
                 for h in range(blk // LANES)], axis=1)
            acc_ref[rows, :] += jnp.dot(a.astype(BF16), vb_ref[pl.ds(ks, blk), :], preferred_element_type=F32)
            carry_ref[rows, :] = carry + jnp.broadcast_to(csum[:, 0:1], carry.shape)

    tiles([qi], True)

    def body(step, c):
        j = qi - 1 - 2 * step
        tiles([j, j - 1], False)
        return c

    lax.fori_loop(0, qi // 2, body, 0)

    @pl.when(qi % 2 == 1)
    def _():
        tiles([0], False)

    o_ref[0] = jnp.where(head0, acc_ref[:blk], acc_ref[blk:]).astype(BF16)


def _stickbreak(proj3, q_norm_g, k_norm_g, *, pool_width, attn_width, blk):
    b, s, _ = proj3.shape
    pairs = attn_width // LANES
    qoff = pool_width // LANES
    koff = qoff + pairs
    voff = koff + pairs
    qg2 = jnp.concatenate([q_norm_g, q_norm_g]).reshape(1, LANES).astype(F32)
    kg2 = jnp.concatenate([k_norm_g, k_norm_g]).reshape(1, LANES).astype(F32)
    li = jnp.arange(LANES) // HEAD_DIM
    bd = (li[:, None] == li[None, :]).astype(BF16)
    tr = (jnp.arange(blk)[:, None] >= jnp.arange(blk)[None, :]).astype(BF16)
    tri = jnp.concatenate([tr, tr], axis=0)
    return pl.pallas_call(
        functools.partial(_stickbreak_kernel, blk=blk),
        grid=(b, pairs, s // blk),
        in_specs=[
            pl.BlockSpec((1, blk, LANES), lambda i, h, j: (i, j, qoff + h)),
            pl.BlockSpec((1, s, LANES), lambda i, h, j: (i, 0, koff + h)),
            pl.BlockSpec((1, s, LANES), lambda i, h, j: (i, 0, voff + h)),
            pl.BlockSpec((1, LANES), lambda i, h, j: (0, 0)),
            pl.BlockSpec((1, LANES), lambda i, h, j: (0, 0)),
            pl.BlockSpec((LANES, LANES), lambda i, h, j: (0, 0)),
            pl.BlockSpec((2 * blk, blk), lambda i, h, j: (0, 0)),
        ],
        out_specs=pl.BlockSpec((1, blk, LANES), lambda i, h, j: (i, j, h)),
        out_shape=jax.ShapeDtypeStruct((b, s, attn_width), BF16),
        scratch_shapes=[
            pltpu.VMEM((s, LANES), BF16),
            pltpu.VMEM((s, LANES), BF16),
            pltpu.VMEM((2 * blk, LANES), BF16),
            pltpu.VMEM((2 * blk, LANES), F32),
            pltpu.VMEM((2 * blk, LANES), F32),
        ],
        compiler_params=_params("parallel", "parallel", "arbitrary"),
        name="stickbreak",
    )(proj3, proj3, proj3, qg2, kg2, bd, tri)


def _mix_router_kernel(pool_ref, attn_ref, x_ref, wp_ref, wa_ref, g_ref, rw_ref, rb_ref, lt_ref,
                       h_ref, hn_ref, te_ref, tg_ref, rk_ref, cnt_ref, run_ref, *, n_exp):
    i = pl.program_id(0)

    @pl.when(i == 0)
    def _():
        run_ref[...] = jnp.zeros_like(run_ref)

    h = (x_ref[...]
         + jnp.dot(pool_ref[...], wp_ref[...], preferred_element_type=F32)
         + jnp.dot(attn_ref[...], wa_ref[...], preferred_element_type=F32))
    h_ref[...] = h
    hn = _rms(h, g_ref[...])
    _store_row_tiles(hn_ref, 0, _pack_halves(hn))
    logits = lax.dot_general(rw_ref[...], hn.astype(BF16), (((1,), (1,)), ((), ())),
                             preferred_element_type=F32) + rb_ref[...]
    tm = logits.shape[1]
    eidx = lax.broadcasted_iota(I32, (n_exp, tm), 0).astype(F32)
    vals, idxs, hots = [], [], []
    rem = logits
    for _ in range(TOP_K):
        m = jnp.max(rem, axis=0, keepdims=True)
        first = jnp.min(jnp.where(rem == m, eidx, float(n_exp)), axis=0, keepdims=True)
        hot = eidx == first
        vals.append(m)
        idxs.append(first)
        hots.append(hot)
        rem = jnp.where(hot, -jnp.inf, rem)
    ex = [jnp.exp(v - vals[0]) for v in vals]
    den = ex[0] + ex[1] + ex[2] + ex[3]
    tg_ref[...] = jnp.concatenate([e / den for e in ex], axis=0)
    te_ref[...] = jnp.concatenate(idxs, axis=0).astype(I32)
    hot_all = (hots[0] | hots[1] | hots[2] | hots[3]).astype(F32)
    before = jnp.dot(hot_all.astype(BF16), lt_ref[...], preferred_element_type=F32)
    base = run_ref[:, 0:1] + before
    rk_ref[...] = jnp.concatenate(
        [jnp.sum(jnp.where(hot, base, 0.0), axis=0, keepdims=True) for hot in hots], axis=0).astype(I32)
    run_ref[...] = run_ref[...] + jnp.sum(hot_all, axis=1, keepdims=True)
    cnt_ref[...] = run_ref[...].astype(I32)


def _mix_router(pool2, attn2, x2, w_pool, w_attn, g, rw_t, rb, *, tm):
    t, d = x2.shape
    half = pool2.shape[1]
    n_exp = rw_t.shape[0]
    lt = (jnp.arange(tm)[:, None] < jnp.arange(tm)[None, :]).astype(BF16)
    row = lambda i: (i, 0)
    fix = lambda i: (0, 0)
    col = lambda i: (0, i)
    return pl.pallas_call(
        functools.partial(_mix_router_kernel, n_exp=n_exp),
        grid=(t // tm,),
        in_specs=[
            pl.BlockSpec((tm, half), row),
            pl.BlockSpec((tm, half), row),
            pl.BlockSpec((tm, d), row),
            pl.BlockSpec((half, d), fix),
            pl.BlockSpec((half, d), fix),
            pl.BlockSpec((1, d), fix),
            pl.BlockSpec((n_exp, d), fix),
            pl.BlockSpec((n_exp, 1), fix),
            pl.BlockSpec((tm, tm), fix),
        ],
        out_specs=[
            pl.BlockSpec((tm, d), row),
            pl.BlockSpec((tm * (d // 2 // LANES), LANES), row),
            pl.BlockSpec((TOP_K, tm), col),
            pl.BlockSpec((TOP_K, tm), col),
            pl.BlockSpec((TOP_K, tm), col),
            pl.BlockSpec((n_exp, LANES), fix),
        ],
        out_shape=[
            jax.ShapeDtypeStruct((t, d), F32),
            jax.ShapeDtypeStruct((t * (d // 2 // LANES), LANES), U32),
            jax.ShapeDtypeStruct((TOP_K, t), I32),
            jax.ShapeDtypeStruct((TOP_K, t), F32),
            jax.ShapeDtypeStruct((TOP_K, t), I32),
            jax.ShapeDtypeStruct((n_exp, LANES), I32),
        ],
        scratch_shapes=[pltpu.VMEM((n_exp, LANES), F32)],
        compiler_params=_params("arbitrary"),
        name="mix_router",
    )(pool2, attn2, x2, w_pool, w_attn, g, rw_t, rb, lt)


def _split_gate_up_kernel(w_ref, p_ref, g_ref, u_ref):
    for c in range(w_ref.shape[2] // (2 * LANES)):
        w = w_ref[0, :, c * 2 * LANES:(c + 1) * 2 * LANES].astype(BF16)
        y = jnp.dot(w, p_ref[...], preferred_element_type=F32)
        g_ref[0, :, c * LANES:(c + 1) * LANES] = y[:, :LANES].astype(BF16)
        u_ref[0, :, c * LANES:(c + 1) * LANES] = y[:, LANES:].astype(BF16)


def _split_gate_up(w_gate_up, *, td):
    n_exp, d, two_f = w_gate_up.shape
    hidden = two_f // 2
    src = jnp.arange(2 * LANES)[:, None]
    col = jnp.arange(2 * LANES)[None, :]
    pick = (src == jnp.where(col < LANES, 2 * col, 2 * (col - LANES) + 1)).astype(BF16)
    out = jax.ShapeDtypeStruct((n_exp, d, hidden), BF16)
    return pl.pallas_call(
        _split_gate_up_kernel,
        grid=(n_exp, d // td),
        in_specs=[
            pl.BlockSpec((1, td, two_f), lambda e, i: (e, i, 0)),
            pl.BlockSpec((2 * LANES, 2 * LANES), lambda e, i: (0, 0)),
        ],
        out_specs=[pl.BlockSpec((1, td, hidden), lambda e, i: (e, i, 0))] * 2,
        out_shape=[out, out],
        compiler_params=_params("parallel", "parallel"),
        name="split_gate_up",
    )(w_gate_up, pick)


def _moe_kernel(ue_ref, un_ref, tokn_ref, dstp_ref, hn_hbm, wg_ref, wu_ref, wd_ref, bg_ref, bu_ref, bd_ref,
                out_hbm, xf_ref, xb_ref, acc_ref, stage_ref, gsem, ssem, *, chunk, trash_row):
    u = pl.program_id(0)
    f = pl.program_id(1)
    nu = pl.num_programs(0)
    nf = pl.num_programs(1)
    subs = UNIT_ROWS // SUB_ROWS
    half = xb_ref.shape[1] // 2
    nsub = (un_ref[u] + (SUB_ROWS - 1)) // SUB_ROWS
    live = un_ref[u] > 0
    prev_live = jnp.logical_and(u > 0, un_ref[jnp.maximum(u - 1, 0)] > 0)
    next_live = jnp.logical_and(u + 1 < nu, un_ref[jnp.minimum(u + 1, nu - 1)] > 0)
    gathered = jnp.logical_or(live, prev_live)
    gathers_next = jnp.logical_or(live, next_live)
    scatters_prev = jnp.logical_or(live, prev_live)

    m = half // LANES

    def whole_gather():
        return pltpu.make_async_copy(hn_hbm.at[pl.ds(0, UNIT_ROWS * m)], xf_ref, gsem)

    def whole_scatter():
        return pltpu.make_async_copy(stage_ref, out_hbm.at[pl.ds(0, UNIT_ROWS * m)], ssem)

    @pl.when(jnp.logical_and(u == 0, f == 0))
    def _():
        acc_ref[...] = jnp.zeros_like(acc_ref)
        stage_ref[...] = jnp.zeros_like(stage_ref)

    @pl.when(jnp.logical_and(jnp.logical_and(u > 0, f == 0), gathered))
    def _():
        whole_gather().wait()
        for s in range(subs):
            rows = slice(s * SUB_ROWS, (s + 1) * SUB_ROWS)
            lo, hi = _unpack_halves(_load_row_tiles(xf_ref, s * SUB_ROWS, SUB_ROWS, m))
            xb_ref[rows, :half] = lo.astype(BF16)
            xb_ref[rows, half:] = hi.astype(BF16)

    def gather_chunk(c):
        base = pl.multiple_of(c * chunk, chunk)
        for j in range(chunk):
            r = base + j
            src = pl.multiple_of(tokn_ref[r] * m, m)
            pltpu.make_async_copy(hn_hbm.at[pl.ds(src, m)], xf_ref.at[pl.ds(r * m, m)], gsem).start()

    def scatter_chunk(c):
        base = pl.multiple_of(c * chunk, chunk)
        for j in range(chunk):
            r = base + j
            dst = pl.multiple_of(dstp_ref[r] * m, m)
            pltpu.make_async_copy(stage_ref.at[pl.ds(r * m, m)], out_hbm.at[pl.ds(dst, m)], ssem).start()

    def expert_mlp(sub_blocks):
        for s in sub_blocks:
            gather_chunk(f * subs + s)
            scatter_chunk(f * subs + s)
        starts = [pl.multiple_of(s * SUB_ROWS, SUB_ROWS) for s in sub_blocks]
        pre = []
        for r0 in starts:
            xs = xb_ref[pl.ds(r0, SUB_ROWS), :]
            pre.append((jnp.dot(xs, wg_ref[0], preferred_element_type=F32),
                        jnp.dot(xs, wu_ref[0], preferred_element_type=F32)))
        wd = wd_ref[0].astype(BF16)
        for r0, (gate, up) in zip(starts, pre):
            gate = jnp.minimum(gate + bg_ref[0], SWIGLU_LIMIT)
            up = jnp.clip(up + bu_ref[0], -SWIGLU_LIMIT, SWIGLU_LIMIT)
            act = gate * jax.nn.sigmoid(SWIGLU_ALPHA * gate) * (up + 1.0)
            y = jnp.dot(act.astype(BF16), wd, preferred_element_type=F32)
            acc_ref[pl.ds(r0, SUB_ROWS), :] = y + jnp.where(f > 0, acc_ref[pl.ds(r0, SUB_ROWS), :], 0.0)

    def pair(i, c):
        expert_mlp([2 * i, 2 * i + 1])
        return c

    lax.fori_loop(0, nsub // 2, pair, 0)

    @pl.when(nsub % 2 == 1)
    def _():
        expert_mlp([nsub - 1])

    def rest(s, c):
        @pl.when(gathers_next)
        def _():
            gather_chunk(f * subs + s)

        @pl.when(scatters_prev)
        def _():
            scatter_chunk(f * subs + s)

        return c

    lax.fori_loop(nsub, subs, rest, 0)

    @pl.when(jnp.logical_and(f == nf - 1, scatters_prev))
    def _():
        whole_scatter().wait()

    @pl.when(jnp.logical_and(f == nf - 1, live))
    def _():
        for s in range(subs):
            rows = slice(s * SUB_ROWS, (s + 1) * SUB_ROWS)
            _store_row_tiles(stage_ref, s * SUB_ROWS, _pack_halves(acc_ref[rows, :] + bd_ref[0]))

    @pl.when(jnp.logical_and(u == nu - 1, f == nf - 1))
    def _():
        stage_ref[...] = jnp.zeros_like(stage_ref)
        fill = pltpu.make_async_copy(stage_ref, out_hbm.at[pl.ds(trash_row * m, UNIT_ROWS * m)], ssem)
        fill.start()
        fill.wait()


def _moe_experts(hn, unit_e, unit_n, row_tok, row_dst, wg, wu, wd, bg, bu, bd, *, n_slots, tf):
    d = wd.shape[2]
    m = d // 2 // LANES
    assert hn.shape[1] == LANES and hn.dtype == U32
    n_units = unit_e.shape[0]
    hidden = wg.shape[2]
    nf = hidden // tf
    subs = UNIT_ROWS // SUB_ROWS
    assert UNIT_ROWS % (nf * subs) == 0
    nxt = lambda u, f, ue, un: (jnp.minimum(u + 1, n_units - 1),)
    prv = lambda u, f, ue, un: (jnp.maximum(u - 1, 0),)
    grid_spec = pltpu.PrefetchScalarGridSpec(
        num_scalar_prefetch=2,
        grid=(n_units, nf),
        in_specs=[
            pl.BlockSpec((UNIT_ROWS,), nxt, memory_space=pltpu.SMEM),
            pl.BlockSpec((UNIT_ROWS,), prv, memory_space=pltpu.SMEM),
            pl.BlockSpec(memory_space=pl.ANY),
            pl.BlockSpec((1, d, tf), lambda u, f, ue, un: (ue[u], 0, f)),
            pl.BlockSpec((1, d, tf), lambda u, f, ue, un: (ue[u], 0, f)),
            pl.BlockSpec((1, tf, d), lambda u, f, ue, un: (ue[u], f, 0)),
            pl.BlockSpec((1, 1, tf), lambda u, f, ue, un: (ue[u], 0, f)),
            pl.BlockSpec((1, 1, tf), lambda u, f, ue, un: (ue[u], 0, f)),
            pl.BlockSpec((1, 1, d), lambda u, f, ue, un: (ue[u], 0, 0)),
        ],
        out_specs=pl.BlockSpec(memory_space=pl.ANY),
        scratch_shapes=[
            pltpu.VMEM((UNIT_ROWS * m, LANES), U32),
            pltpu.VMEM((UNIT_ROWS, d), BF16),
            pltpu.VMEM((UNIT_ROWS, d), F32),
            pltpu.VMEM((UNIT_ROWS * m, LANES), U32),
            pltpu.SemaphoreType.DMA(()),
            pltpu.SemaphoreType.DMA(()),
        ],
    )
    return pl.pallas_call(
        functools.partial(_moe_kernel, chunk=UNIT_ROWS // (nf * subs), trash_row=n_slots - UNIT_ROWS),
        grid_spec=grid_spec,
        out_shape=jax.ShapeDtypeStruct((n_slots * m, LANES), U32),
        compiler_params=_params("arbitrary", "arbitrary"),
        name="moe_experts",
    )(unit_e, unit_n, row_tok, row_dst, hn, wg, wu, wd, bg, bu, bd)


def _ple_kernel(h_ref, y0_ref, y1_ref, y2_ref, y3_ref, tg_ref, p_ref, g1_ref, wg_ref, wp_ref, g2_ref, o_ref):
    tm, d = h_ref.shape
    n = tm // 2
    hs, gates, embeds = [], [], []
    for first in (0, n):
        rows = slice(first, first + n)
        lo = h_ref[rows, :d // 2]
        hi = h_ref[rows, d // 2:]
        for k, y_ref in enumerate((y0_ref, y1_ref, y2_ref, y3_ref)):
            y_lo, y_hi = _unpack_halves(_load_row_tiles(y_ref, first, n, d // 2 // LANES))
            lo = lo + y_lo * tg_ref[rows, k:k + 1]
            hi = hi + y_hi * tg_ref[rows, k:k + 1]
        h = jnp.concatenate([lo, hi], axis=1)
        hs.append(h)
        gates.append(jnp.dot(_rms(h, g1_ref[...]).astype(BF16), wg_ref[...], preferred_element_type=F32))
        embeds.append(jnp.dot(p_ref[rows, :].astype(BF16), wp_ref[...], preferred_element_type=F32))
    for first, h, gate, e in zip((0, n), hs, gates, embeds):
        o_ref[first:first + n, :] = h + jax.nn.sigmoid(gate) * _rms(e, g2_ref[...])


def _ple_out(h1, slots, gates_t, p2, g1, w_gate, w_proj, g2, *, tm):
    t, d = h1.shape
    pd = p2.shape[1]
    assert TOP_K == 4
    row = lambda i: (i, 0)
    fix = lambda i: (0, 0)
    m = d // 2 // LANES
    plane = lambda k: pl.BlockSpec((tm * m, LANES), lambda i: (k * (t // tm) + i, 0))
    return pl.pallas_call(
        _ple_kernel,
        grid=(t // tm,),
        in_specs=[
            pl.BlockSpec((tm, d), row),
            plane(0), plane(1), plane(2), plane(3),
            pl.BlockSpec((tm, TOP_K), row),
            pl.BlockSpec((tm, pd), row),
            pl.BlockSpec((1, d), fix),
            pl.BlockSpec((d, d), fix),
            pl.BlockSpec((pd, d), fix),
            pl.BlockSpec((1, d), fix),
        ],
        out_specs=pl.BlockSpec((tm, d), row),
        out_shape=jax.ShapeDtypeStruct((t, d), F32),
        compiler_params=_params("parallel"),
        name="ple_out",
    )(h1, slots, slots, slots, slots, gates_t, p2, g1, w_gate, w_proj, g2)


def _routing_tables(top_e, rank, counts, *, n_units, n_tok):
    n_exp = counts.shape[0]
    experts = jnp.arange(n_exp, dtype=I32)
    units_per = (counts + (UNIT_ROWS - 1)) // UNIT_ROWS
    unit_end = 1 + jnp.cumsum(units_per)
    unit_start = unit_end - units_per
    uid = jnp.arange(n_units, dtype=I32)
    unit_e = jnp.minimum(jnp.sum(unit_end[None, :] <= uid[:, None], axis=1), n_exp - 1).astype(I32)
    mine = unit_e[:, None] == experts[None, :]
    first = jnp.sum(jnp.where(mine, unit_start[None, :], 0), axis=1)
    total = jnp.sum(jnp.where(mine, counts[None, :], 0), axis=1)
    unit_n = jnp.where(uid >= 1, jnp.clip(total - (uid - first) * UNIT_ROWS, 0, UNIT_ROWS), 0).astype(I32)
    start = jnp.sum(jnp.where(top_e[:, :, None] == experts, unit_start, 0), axis=-1)
    row = (start + rank // UNIT_ROWS) * UNIT_ROWS + rank % UNIT_ROWS
    slot = jnp.arange(TOP_K, dtype=I32)[:, None] * n_tok + jnp.arange(n_tok, dtype=I32)[None, :]
    n_rows = n_units * UNIT_ROWS
    row_slot = jnp.full((n_rows,), -1, I32).at[row.reshape(-1)].set(slot.reshape(-1), unique_indices=True)
    valid = row_slot >= 0
    row_tok = jnp.where(valid, row_slot % n_tok, 0).astype(I32)
    trash = TOP_K * n_tok + jnp.arange(n_rows, dtype=I32) % UNIT_ROWS
    row_dst = jnp.where(valid, row_slot, trash).astype(I32)
    return unit_e, unit_n, row_tok, row_dst


def _tile(n, want):
    want = min(n, want)
    assert n % want == 0, (n, want)
    return want


def _layer(h, p_i, ln_mix_g, w_in, pool_w, pool_scale, q_norm_g, k_norm_g, w_out, ln_ffn_g, router_w, router_b,
           w_gate_up, b_gate_up, w_down, b_down, ln_ple_g, ple_gate_w, ple_proj, ple_norm_g):
    b, s, d = h.shape
    t = b * s
    pool_width = d // 2
    attn_width = d // 2
    n_exp = router_w.shape[1]
    hidden = w_down.shape[1]
    assert pool_width % (LANES * len(POOL_WINDOWS)) == 0 and attn_width % LANES == 0
    x2 = h.reshape(t, d)

    proj = _in_proj(x2, ln_mix_g.reshape(1, d), w_in.astype(BF16), tm=_tile(t, 1024), tn=_tile(w_in.shape[1], 1024))
    proj3 = proj.reshape(b, s, -1)
    pool_out = _pool_mixer(proj3, pool_w.astype(BF16), pool_scale.reshape(1, pool_width), width=pool_width)
    attn_out = _stickbreak(proj3, q_norm_g, k_norm_g, pool_width=pool_width, attn_width=attn_width,
                           blk=_tile(s, 256))

    w_out_b = w_out.astype(BF16)
    h1, hn, top_e, gates, rank, counts = _mix_router(
        pool_out.reshape(t, pool_width), attn_out.reshape(t, attn_width), x2,
        w_out_b[:pool_width], w_out_b[pool_width:], ln_ffn_g.reshape(1, d),
        router_w.T.astype(BF16), router_b.reshape(n_exp, 1), tm=_tile(t, 512))

    n_units = t * TOP_K // UNIT_ROWS + n_exp + 2
    unit_e, unit_n, row_tok, row_dst = _routing_tables(top_e, rank, counts[:, 0], n_units=n_units, n_tok=t)
    wg, wu = _split_gate_up(w_gate_up, td=_tile(d, 256))
    bg = b_gate_up[:, 0::2].reshape(n_exp, 1, hidden)
    bu = b_gate_up[:, 1::2].reshape(n_exp, 1, hidden)
    slots = _moe_experts(hn, unit_e, unit_n, row_tok, row_dst, wg, wu, w_down, bg, bu,
                         b_down.reshape(n_exp, 1, d), n_slots=t * TOP_K + UNIT_ROWS, tf=_tile(hidden, 512))

    out = _ple_out(h1, slots, gates.T, p_i.reshape(t, -1),
                   ln_ple_g.reshape(1, d), ple_gate_w.astype(BF16), ple_proj.astype(BF16),
                   ple_norm_g.reshape(1, d), tm=_tile(t, 256))
    return out.reshape(b, s, d)


def kernel(x, p, ln_mix_g, w_in, pool_w, pool_scale, q_norm_g, k_norm_g, w_out, ln_ffn_g, router_w, router_b,
           w_gate_up, b_gate_up, w_down, b_down, ln_ple_g, ple_gate_w, ple_proj, ple_norm_g):
    h = x
    for i in range(p.shape[0]):
        h = _layer(h, p[i], ln_mix_g[i], w_in[i], pool_w[i], pool_scale[i], q_norm_g[i], k_norm_g[i], w_out[i],
                   ln_ffn_g[i], router_w[i], router_b[i], w_gate_up[i], b_gate_up[i], w_down[i], b_down[i],
                   ln_ple_g[i], ple_gate_w[i], ple_proj[i], ple_norm_g[i])
    return h
```

```python
import functools

import jax
import jax.numpy as jnp
from jax import lax
from jax.experimental import pallas as pl
from jax.experimental.pallas import tpu as pltpu

F32 = jnp.float32
BF16 = jnp.bfloat16
I32 = jnp.int32
U32 = jnp.uint32

HEAD_DIM = 64
POOL_WINDOWS = (2, 4, 8, 16)
TOP_K = 4
SWIGLU_LIMIT = 7.0
SWIGLU_ALPHA = 1.702
EPS = 1e-6
LOG2E = 1.4426950408889634

LANES = 128
VMEM_LIMIT = 56 * 1024 * 1024
UNIT_ROWS = 1024
SUB_ROWS = 256


def _params(*sem):
    return pltpu.CompilerParams(dimension_semantics=sem, vmem_limit_bytes=VMEM_LIMIT)


def _rms(x, g):
    return x * lax.rsqrt(jnp.mean(x * x, axis=-1, keepdims=True) + EPS) * g


def _pack_halves(x):
    n = x.shape[1] // 2
    lo = lax.bitcast_convert_type(x[:, :n].astype(BF16).astype(F32), U32)
    hi = lax.bitcast_convert_type(x[:, n:].astype(BF16).astype(F32), U32)
    return hi | (lo >> 16)


def _unpack_halves(w):
    lo = lax.bitcast_convert_type(w << 16, F32)
    hi = lax.bitcast_convert_type(w & jnp.uint32(0xFFFF0000), F32)
    return lo, hi


def _store_row_tiles(ref, first, packed):
    n, width = packed.shape
    m = width // LANES
    for c in range(m):
        ref[pl.ds(first * m + c, n, stride=m), :] = packed[:, c * LANES:(c + 1) * LANES]


def _load_row_tiles(ref, first, n, m):
    return jnp.concatenate([ref[pl.ds(first * m + c, n, stride=m), :] for c in range(m)], axis=1)


def _split_bf16(x):
    hi = x.astype(BF16)
    lo = (x - hi.astype(F32)).astype(BF16)
    return hi, lo


def _in_proj_kernel(x_ref, g_ref, w_ref, o_ref, xn_ref):
    @pl.when(pl.program_id(1) == 0)
    def _():
        xn_ref[...] = _rms(x_ref[...], g_ref[...]).astype(BF16)

    o_ref[...] = jnp.dot(xn_ref[...], w_ref[...], preferred_element_type=F32)


def _in_proj(x2, g, w, *, tm, tn):
    t, d = x2.shape
    n = w.shape[1]
    return pl.pallas_call(
        _in_proj_kernel,
        grid=(t // tm, n // tn),
        in_specs=[
            pl.BlockSpec((tm, d), lambda i, j: (i, 0)),
            pl.BlockSpec((1, d), lambda i, j: (0, 0)),
            pl.BlockSpec((d, tn), lambda i, j: (0, j)),
        ],
        out_specs=pl.BlockSpec((tm, tn), lambda i, j: (i, j)),
        out_shape=jax.ShapeDtypeStruct((t, n), F32),
        scratch_shapes=[pltpu.VMEM((tm, d), BF16)],
        compiler_params=_params("parallel", "arbitrary"),
        name="in_proj",
    )(x2, g, w)


def _pool_kernel(u_ref, w_ref, s_ref, o_ref, *, group):
    seq = u_ref.shape[1]
    row = lax.broadcasted_iota(I32, (seq, 1), 0)
    for gi, win in enumerate(POOL_WINDOWS):
        sl = slice(gi * group, (gi + 1) * group)
        u = u_ref[0, :, sl]
        acc = u
        span = 1
        while span < win:
            shifted = pltpu.roll(acc, span, axis=0)
            acc = acc + jnp.where(row >= span, shifted, 0.0)
            span *= 2
        count = jnp.minimum(row + 1, win).astype(F32)
        d = (acc / count - u).astype(BF16)
        y = jnp.dot(d, w_ref[gi], preferred_element_type=F32)
        o_ref[0, :, sl] = (y * s_ref[:, sl]).astype(BF16)


def _pool_mixer(proj3, pool_w, pool_scale, *, width):
    b, s, _ = proj3.shape
    group = width // len(POOL_WINDOWS)
    return pl.pallas_call(
        functools.partial(_pool_kernel, group=group),
        grid=(b,),
        in_specs=[
            pl.BlockSpec((1, s, width), lambda i: (i, 0, 0)),
            pl.BlockSpec((len(POOL_WINDOWS), group, group), lambda i: (0, 0, 0)),
            pl.BlockSpec((1, width), lambda i: (0, 0)),
        ],
        out_specs=pl.BlockSpec((1, s, width), lambda i: (i, 0, 0)),
        out_shape=jax.ShapeDtypeStruct((b, s, width), BF16),
        compiler_params=_params("parallel"),
        name="pool_mixer",
    )(proj3, pool_w, pool_scale)


def _head_rmsnorm(x, g2, bd):
    hi, lo = _split_bf16(x * x)
    ss = jnp.dot(hi, bd, preferred_element_type=F32) + jnp.dot(lo, bd, preferred_element_type=F32)
    return x * lax.rsqrt(ss * (1.0 / HEAD_DIM) + EPS) * g2


def _stickbreak_kernel(q_ref, k_ref, v_ref, qg_ref, kg_ref, bd_ref, tri_ref, o_ref,
                       kn_ref, vb_ref, q2_ref, carry_ref, acc_ref, *, blk):
    qi = pl.program_id(2)
    lane = lax.broadcasted_iota(I32, (1, LANES), 1)
    head0 = lane < HEAD_DIM

    @pl.when(qi == 0)
    def _():
        kn_ref[...] = _head_rmsnorm(k_ref[0], kg_ref[...], bd_ref[...]).astype(BF16)
        vb_ref[...] = v_ref[0].astype(BF16)

    q = _head_rmsnorm(q_ref[0], qg_ref[...], bd_ref[...]) * (HEAD_DIM ** -0.5 * LOG2E)
    q2_ref[...] = jnp.concatenate([jnp.where(head0, q, 0.0), jnp.where(head0, 0.0, q)], axis=0).astype(BF16)
    carry_ref[...] = jnp.zeros_like(carry_ref)
    acc_ref[...] = jnp.zeros_like(acc_ref)

    heads = (slice(0, blk), slice(blk, 2 * blk))

    def tiles(blocks, masked):
        starts = [j * blk if isinstance(j, int) else pl.multiple_of(j * blk, blk) for j in blocks]
        chains = [(ks, rows) for ks in starts for rows in heads]
        zs = [lax.dot_general(q2_ref[rows, :], kn_ref[pl.ds(ks, blk), :], (((1,), (1,)), ((), ())),
                              preferred_element_type=F32) for ks, rows in chains]
        csums = []
        for i, z in enumerate(zs):
            sp = jnp.maximum(z, 0.0) + jnp.log2(1.0 + jnp.exp2(jnp.minimum(z, -z)))
            if masked:
                r = lax.broadcasted_iota(I32, (blk, blk), 0)
                c = lax.broadcasted_iota(I32, (blk, blk), 1)
                keep = c < r
                sp = jnp.where(keep, sp, 0.0)
                zs[i] = jnp.where(keep, z, -1e30)
            csums.append(jnp.dot(sp.astype(BF16), tri_ref[:blk, :], preferred_element_type=F32))
        for (ks, rows), z, csum in zip(chains, zs, csums):
            carry = carry_ref[rows, :]
            a = jnp.concatenate(
                [jnp.exp2(z[:, h * LANES:(h + 1) * LANES] - csum[:, h * LANES:(h + 1) * LANES] - carry)
                 for h in range(blk // LANES)], axis=1)
            acc_ref[rows, :] += jnp.dot(a.astype(BF16), vb_ref[pl.ds(ks, blk), :], preferred_element_type=F32)
            carry_ref[rows, :] = carry + jnp.broadcast_to(csum[:, 0:1], carry.shape)

    tiles([qi], True)

    def body(step, c):
        j = qi - 1 - 2 * step
        tiles([j, j - 1], False)
        return c

    lax.fori_loop(0, qi // 2, body, 0)

    @pl.when(qi % 2 == 1)
    def _():
        tiles([0], False)

    o_ref[0] = jnp.where(head0, acc_ref[:blk], acc_ref[blk:]).astype(BF16)


def _stickbreak(proj3, q_norm_g, k_norm_g, *, pool_width, attn_width, blk):
    b, s, _ = proj3.shape
    pairs = attn_width // LANES
    qoff = pool_width // LANES
    koff = qoff + pairs
    voff = koff + pairs
    qg2 = jnp.concatenate([q_norm_g, q_norm_g]).reshape(1, LANES).astype(F32)
    kg2 = jnp.concatenate([k_norm_g, k_norm_g]).reshape(1, LANES).astype(F32)
    li = jnp.arange(LANES) // HEAD_DIM
    bd = (li[:, None] == li[None, :]).astype(BF16)
    tr = (jnp.arange(blk)[:, None] >= jnp.arange(blk)[None, :]).astype(BF16)
    tri = jnp.concatenate([tr, tr], axis=0)
    return pl.pallas_call(
        functools.partial(_stickbreak_kernel, blk=blk),
        grid=(b, pairs, s // blk),
        in_specs=[
            pl.BlockSpec((1, blk, LANES), lambda i, h, j: (i, j, qoff + h)),
            pl.BlockSpec((1, s, LANES), lambda i, h, j: (i, 0, koff + h)),
            pl.BlockSpec((1, s, LANES), lambda i, h, j: (i, 0, voff + h)),
            pl.BlockSpec((1, LANES), lambda i, h, j: (0, 0)),
            pl.BlockSpec((1, LANES), lambda i, h, j: (0, 0)),
            pl.BlockSpec((LANES, LANES), lambda i, h, j: (0, 0)),
            pl.BlockSpec((2 * blk, blk), lambda i, h, j: (0, 0)),
        ],
        out_specs=pl.BlockSpec((1, blk, LANES), lambda i, h, j: (i, j, h)),
        out_shape=jax.ShapeDtypeStruct((b, s, attn_width), BF16),
        scratch_shapes=[
            pltpu.VMEM((s, LANES), BF16),
            pltpu.VMEM((s, LANES), BF16),
            pltpu.VMEM((2 * blk, LANES), BF16),
            pltpu.VMEM((2 * blk, LANES), F32),
            pltpu.VMEM((2 * blk, LANES), F32),
        ],
        compiler_params=_params("parallel", "parallel", "arbitrary"),
        name="stickbreak",
    )(proj3, proj3, proj3, qg2, kg2, bd, tri)


def _mix_router_kernel(pool_ref, attn_ref, x_ref, wp_ref, wa_ref, g_ref, rw_ref, rb_ref, lt_ref,
                       h_ref, hn_ref, te_ref, tg_ref, rk_ref, cnt_ref, run_ref, *, n_exp):
    i = pl.program_id(0)

    @pl.when(i == 0)
    def _():
        run_ref[...] = jnp.zeros_like(run_ref)

    h = (x_ref[...]
         + jnp.dot(pool_ref[...], wp_ref[...], preferred_element_type=F32)
         + jnp.dot(attn_ref[...], wa_ref[...], preferred_element_type=F32))
    h_ref[...] = h
    hn = _rms(h, g_ref[...])
    _store_row_tiles(hn_ref, 0, _pack_halves(hn))
    logits = lax.dot_general(rw_ref[...], hn.astype(BF16), (((1,), (1,)), ((), ())),
                             preferred_element_type=F32) + rb_ref[...]
    tm = logits.shape[1]
    eidx = lax.broadcasted_iota(I32, (n_exp, tm), 0).astype(F32)
    vals, idxs, hots = [], [], []
    rem = logits
    for _ in range(TOP_K):
        m = jnp.max(rem, axis=0, keepdims=True)
        first = jnp.min(jnp.where(rem == m, eidx, float(n_exp)), axis=0, keepdims=True)
        hot = eidx == first
        vals.append(m)
        idxs.append(first)
        hots.append(hot)
        rem = jnp.where(hot, -jnp.inf, rem)
    ex = [jnp.exp(v - vals[0]) for v in vals]
    den = ex[0] + ex[1] + ex[2] + ex[3]
    tg_ref[...] = jnp.concatenate([e / den for e in ex], axis=0)
    te_ref[...] = jnp.concatenate(idxs, axis=0).astype(I32)
    hot_all = (hots[0] | hots[1] | hots[2] | hots[3]).astype(F32)
    before = jnp.dot(hot_all.astype(BF16), lt_ref[...], preferred_element_type=F32)
    base = run_ref[:, 0:1] + before
    rk_ref[...] = jnp.concatenate(
        [jnp.sum(jnp.where(hot, base, 0.0), axis=0, keepdims=True) for hot in hots], axis=0).astype(I32)
    run_ref[...] = run_ref[...] + jnp.sum(hot_all, axis=1, keepdims=True)
    cnt_ref[...] = run_ref[...].astype(I32)


def _mix_router(pool2, attn2, x2, w_pool, w_attn, g, rw_t, rb, *, tm):
    t, d = x2.shape
    half = pool2.shape[1]
    n_exp = rw_t.shape[0]
    lt = (jnp.arange(tm)[:, None] < jnp.arange(tm)[None, :]).astype(BF16)
    row = lambda i: (i, 0)
    fix = lambda i: (0, 0)
    col = lambda i: (0, i)
    return pl.pallas_call(
        functools.partial(_mix_router_kernel, n_exp=n_exp),
        grid=(t // tm,),
        in_specs=[
            pl.BlockSpec((tm, half), row),
            pl.BlockSpec((tm, half), row),
            pl.BlockSpec((tm, d), row),
            pl.BlockSpec((half, d), fix),
            pl.BlockSpec((half, d), fix),
            pl.BlockSpec((1, d), fix),
            pl.BlockSpec((n_exp, d), fix),
            pl.BlockSpec((n_exp, 1), fix),
            pl.BlockSpec((tm, tm), fix),
        ],
        out_specs=[
            pl.BlockSpec((tm, d), row),
            pl.BlockSpec((tm * (d // 2 // LANES), LANES), row),
            pl.BlockSpec((TOP_K, tm), col),
            pl.BlockSpec((TOP_K, tm), col),
            pl.BlockSpec((TOP_K, tm), col),
            pl.BlockSpec((n_exp, LANES), fix),
        ],
        out_shape=[
            jax.ShapeDtypeStruct((t, d), F32),
            jax.ShapeDtypeStruct((t * (d // 2 // LANES), LANES), U32),
            jax.ShapeDtypeStruct((TOP_K, t), I32),
            jax.ShapeDtypeStruct((TOP_K, t), F32),
            jax.ShapeDtypeStruct((TOP_K, t), I32),
            jax.ShapeDtypeStruct((n_exp, LANES), I32),
        ],
        scratch_shapes=[pltpu.VMEM((n_exp, LANES), F32)],
        compiler_params=_params("arbitrary"),
        name="mix_router",
    )(pool2, attn2, x2, w_pool, w_attn, g, rw_t, rb, lt)


def _split_gate_up_kernel(w_ref, p_ref, g_ref, u_ref):
    for c in range(w_ref.shape[2] // (2 * LANES)):
        w = w_ref[0, :, c * 2 * LANES:(c + 1) * 2 * LANES].astype(BF16)
        y = jnp.dot(w, p_ref[...], preferred_element_type=F32)
        g_ref[0, :, c * LANES:(c + 1) * LANES] = y[:, :LANES].astype(BF16)
        u_ref[0, :, c * LANES:(c + 1) * LANES] = y[:, LANES:].astype(BF16)


def _split_gate_up(w_gate_up, *, td):
    n_exp, d, two_f = w_gate_up.shape
    hidden = two_f // 2
    src = jnp.arange(2 * LANES)[:, None]
    col = jnp.arange(2 * LANES)[None, :]
    pick = (src == jnp.where(col < LANES, 2 * col, 2 * (col - LANES) + 1)).astype(BF16)
    out = jax.ShapeDtypeStruct((n_exp, d, hidden), BF16)
    return pl.pallas_call(
        _split_gate_up_kernel,
        grid=(n_exp, d // td),
        in_specs=[
            pl.BlockSpec((1, td, two_f), lambda e, i: (e, i, 0)),
            pl.BlockSpec((2 * LANES, 2 * LANES), lambda e, i: (0, 0)),
        ],
        out_specs=[pl.BlockSpec((1, td, hidden), lambda e, i: (e, i, 0))] * 2,
        out_shape=[out, out],
        compiler_params=_params("parallel", "parallel"),
        name="split_gate_up",
    )(w_gate_up, pick)


def _moe_kernel(ue_ref, un_ref, tokn_ref, dstp_ref, hn_hbm, wg_ref, wu_ref, wd_ref, bg_ref, bu_ref, bd_ref,
                out_hbm, xf_ref, xb_ref, acc_ref, stage_ref, gsem, ssem, *, chunk, trash_row):
    u = pl.program_id(0)
    f = pl.program_id(1)
    nu = pl.num_programs(0)
    nf = pl.num_programs(1)
    subs = UNIT_ROWS // SUB_ROWS
    half = xb_ref.shape[1] // 2
    nsub = (un_ref[u] + (SUB_ROWS - 1)) // SUB_ROWS
    live = un_ref[u] > 0
    prev_live = jnp.logical_and(u > 0, un_ref[jnp.maximum(u - 1, 0)] > 0)
    next_live = jnp.logical_and(u + 1 < nu, un_ref[jnp.minimum(u + 1, nu - 1)] > 0)
    gathered = jnp.logical_or(live, prev_live)
    gathers_next = jnp.logical_or(live, next_live)
    scatters_prev = jnp.logical_or(live, prev_live)

    m = half // LANES

    def whole_gather():
        return pltpu.make_async_copy(hn_hbm.at[pl.ds(0, UNIT_ROWS * m)], xf_ref, gsem)

    def whole_scatter():
        return pltpu.make_async_copy(stage_ref, out_hbm.at[pl.ds(0, UNIT_ROWS * m)], ssem)

    @pl.when(jnp.logical_and(u == 0, f == 0))
    def _():
        acc_ref[...] = jnp.zeros_like(acc_ref)
        stage_ref[...] = jnp.zeros_like(stage_ref)

    @pl.when(jnp.logical_and(jnp.logical_and(u > 0, f == 0), gathered))
    def _():
        whole_gather().wait()
        for s in range(subs):
            rows = slice(s * SUB_ROWS, (s + 1) * SUB_ROWS)
            lo, hi = _unpack_halves(_load_row_tiles(xf_ref, s * SUB_ROWS, SUB_ROWS, m))
            xb_ref[rows, :half] = lo.astype(BF16)
            xb_ref[rows, half:] = hi.astype(BF16)

    def gather_chunk(c):
        base = pl.multiple_of(c * chunk, chunk)
        for j in range(chunk):
            r = base + j
            src = pl.multiple_of(tokn_ref[r] * m, m)
            pltpu.make_async_copy(hn_hbm.at[pl.ds(src, m)], xf_ref.at[pl.ds(r * m, m)], gsem).start()

    def scatter_chunk(c):
        base = pl.multiple_of(c * chunk, chunk)
        for j in range(chunk):
            r = base + j
            dst = pl.multiple_of(dstp_ref[r] * m, m)
            pltpu.make_async_copy(stage_ref.at[pl.ds(r * m, m)], out_hbm.at[pl.ds(dst, m)], ssem).start()

    def expert_mlp(sub_blocks):
        for s in sub_blocks:
            gather_chunk(f * subs + s)
            scatter_chunk(f * subs + s)
        starts = [pl.multiple_of(s * SUB_ROWS, SUB_ROWS) for s in sub_blocks]
        pre = []
        for r0 in starts:
            xs = xb_ref[pl.ds(r0, SUB_ROWS), :]
            pre.append((jnp.dot(xs, wg_ref[0], preferred_element_type=F32),
                        jnp.dot(xs, wu_ref[0], preferred_element_type=F32)))
        wd = wd_ref[0].astype(BF16)
        for r0, (gate, up) in zip(starts, pre):
            gate = jnp.minimum(gate + bg_ref[0], SWIGLU_LIMIT)
            up = jnp.clip(up + bu_ref[0], -SWIGLU_LIMIT, SWIGLU_LIMIT)
            act = gate * jax.nn.sigmoid(SWIGLU_ALPHA * gate) * (up + 1.0)
            y = jnp.dot(act.astype(BF16), wd, preferred_element_type=F32)
            acc_ref[pl.ds(r0, SUB_ROWS), :] = y + jnp.where(f > 0, acc_ref[pl.ds(r0, SUB_ROWS), :], 0.0)

    def pair(i, c):
        expert_mlp([2 * i, 2 * i + 1])
        return c

    lax.fori_loop(0, nsub // 2, pair, 0)

    @pl.when(nsub % 2 == 1)
    def _():
        expert_mlp([nsub - 1])

    def rest(s, c):
        @pl.when(gathers_next)
        def _():
            gather_chunk(f * subs + s)

        @pl.when(scatters_prev)
        def _():
            scatter_chunk(f * subs + s)

        return c

    lax.fori_loop(nsub, subs, rest, 0)

    @pl.when(jnp.logical_and(f == nf - 1, scatters_prev))
    def _():
        whole_scatter().wait()

    @pl.when(jnp.logical_and(f == nf - 1, live))
    def _():
        for s in range(subs):
            rows = slice(s * SUB_ROWS, (s + 1) * SUB_ROWS)
            _store_row_tiles(stage_ref, s * SUB_ROWS, _pack_halves(acc_ref[rows, :] + bd_ref[0]))

    @pl.when(jnp.logical_and(u == nu - 1, f == nf - 1))
    def _():
        stage_ref[...] = jnp.zeros_like(stage_ref)
        fill = pltpu.make_async_copy(stage_ref, out_hbm.at[pl.ds(trash_row * m, UNIT_ROWS * m)], ssem)
        fill.start()
        fill.wait()


def _moe_experts(hn, unit_e, unit_n, row_tok, row_dst, wg, wu, wd, bg, bu, bd, *, n_slots, tf):
    d = wd.shape[2]
    m = d // 2 // LANES
    assert hn.shape[1] == LANES and hn.dtype == U32
    n_units = unit_e.shape[0]
    hidden = wg.shape[2]
    nf = hidden // tf
    subs = UNIT_ROWS // SUB_ROWS
    assert UNIT_ROWS % (nf * subs) == 0
    nxt = lambda u, f, ue, un: (jnp.minimum(u + 1, n_units - 1),)
    prv = lambda u, f, ue, un: (jnp.maximum(u - 1, 0),)
    grid_spec = pltpu.PrefetchScalarGridSpec(
        num_scalar_prefetch=2,
        grid=(n_units, nf),
        in_specs=[
            pl.BlockSpec((UNIT_ROWS,), nxt, memory_space=pltpu.SMEM),
            pl.BlockSpec((UNIT_ROWS,), prv, memory_space=pltpu.SMEM),
            pl.BlockSpec(memory_space=pl.ANY),
            pl.BlockSpec((1, d, tf), lambda u, f, ue, un: (ue[u], 0, f)),
            pl.BlockSpec((1, d, tf), lambda u, f, ue, un: (ue[u], 0, f)),
            pl.BlockSpec((1, tf, d), lambda u, f, ue, un: (ue[u], f, 0)),
            pl.BlockSpec((1, 1, tf), lambda u, f, ue, un: (ue[u], 0, f)),
            pl.BlockSpec((1, 1, tf), lambda u, f, ue, un: (ue[u], 0, f)),
            pl.BlockSpec((1, 1, d), lambda u, f, ue, un: (ue[u], 0, 0)),
        ],
        out_specs=pl.BlockSpec(memory_space=pl.ANY),
        scratch_shapes=[
            pltpu.VMEM((UNIT_ROWS * m, LANES), U32),
            pltpu.VMEM((UNIT_ROWS, d), BF16),
            pltpu.VMEM((UNIT_ROWS, d), F32),
            pltpu.VMEM((UNIT_ROWS * m, LANES), U32),
            pltpu.SemaphoreType.DMA(()),
            pltpu.SemaphoreType.DMA(()),
        ],
    )
    return pl.pallas_call(
        functools.partial(_moe_kernel, chunk=UNIT_ROWS // (nf * subs), trash_row=n_slots - UNIT_ROWS),
        grid_spec=grid_spec,
        out_shape=jax.ShapeDtypeStruct((n_slots * m, LANES), U32),
        compiler_params=_params("arbitrary", "arbitrary"),
        name="moe_experts",
    )(unit_e, unit_n, row_tok, row_dst, hn, wg, wu, wd, bg, bu, bd)


def _ple_kernel(h_ref, y0_ref, y1_ref, y2_ref, y3_ref, tg_ref, p_ref, g1_ref, wg_ref, wp_ref, g2_ref, o_ref):
    tm, d = h_ref.shape
    lo = h_ref[:, :d // 2]
    hi = h_ref[:, d // 2:]
    for k, y_ref in enumerate((y0_ref, y1_ref, y2_ref, y3_ref)):
        y_lo, y_hi = _unpack_halves(_load_row_tiles(y_ref, 0, tm, d // 2 // LANES))
        lo = lo + y_lo * tg_ref[:, k:k + 1]
        hi = hi + y_hi * tg_ref[:, k:k + 1]
    h = jnp.concatenate([lo, hi], axis=1)
    gate = jax.nn.sigmoid(jnp.dot(_rms(h, g1_ref[...]).astype(BF16), wg_ref[...], preferred_element_type=F32))
    e = _rms(jnp.dot(p_ref[...].astype(BF16), wp_ref[...], preferred_element_type=F32), g2_ref[...])
    o_ref[...] = h + gate * e


def _ple_out(h1, slots, gates_t, p2, g1, w_gate, w_proj, g2, *, tm):
    t, d = h1.shape
    pd = p2.shape[1]
    assert TOP_K == 4
    row = lambda i: (i, 0)
    fix = lambda i: (0, 0)
    m = d // 2 // LANES
    plane = lambda k: pl.BlockSpec((tm * m, LANES), lambda i: (k * (t // tm) + i, 0))
    return pl.pallas_call(
        _ple_kernel,
        grid=(t // tm,),
        in_specs=[
            pl.BlockSpec((tm, d), row),
            plane(0), plane(1), plane(2), plane(3),
            pl.BlockSpec((tm, TOP_K), row),
            pl.BlockSpec((tm, pd), row),
            pl.BlockSpec((1, d), fix),
            pl.BlockSpec((d, d), fix),
            pl.BlockSpec((pd, d), fix),
            pl.BlockSpec((1, d), fix),
        ],
        out_specs=pl.BlockSpec((tm, d), row),
        out_shape=jax.ShapeDtypeStruct((t, d), F32),
        compiler_params=_params("parallel"),
        name="ple_out",
    )(h1, slots, slots, slots, slots, gates_t, p2, g1, w_gate, w_proj, g2)


def _routing_tables(top_e, rank, counts, *, n_units, n_tok):
    n_exp = counts.shape[0]
    experts = jnp.arange(n_exp, dtype=I32)
    units_per = (counts + (UNIT_ROWS - 1)) // UNIT_ROWS
    unit_end = 1 + jnp.cumsum(units_per)
    unit_start = unit_end - units_per
    uid = jnp.arange(n_units, dtype=I32)
    unit_e = jnp.minimum(jnp.sum(unit_end[None, :] <= uid[:, None], axis=1), n_exp - 1).astype(I32)
    mine = unit_e[:, None] == experts[None, :]
    first = jnp.sum(jnp.where(mine, unit_start[None, :], 0), axis=1)
    total = jnp.sum(jnp.where(mine, counts[None, :], 0), axis=1)
    unit_n = jnp.where(uid >= 1, jnp.clip(total - (uid - first) * UNIT_ROWS, 0, UNIT_ROWS), 0).astype(I32)
    start = jnp.sum(jnp.where(top_e[:, :, None] == experts, unit_start, 0), axis=-1)
    row = (start + rank // UNIT_ROWS) * UNIT_ROWS + rank % UNIT_ROWS
    slot = jnp.arange(TOP_K, dtype=I32)[:, None] * n_tok + jnp.arange(n_tok, dtype=I32)[None, :]
    n_rows = n_units * UNIT_ROWS
    row_slot = jnp.full((n_rows,), -1, I32).at[row.reshape(-1)].set(
        slot.reshape(-1), unique_indices=True, mode="promise_in_bounds")
    valid = row_slot >= 0
    row_tok = jnp.where(valid, row_slot % n_tok, 0).astype(I32)
    trash = TOP_K * n_tok + jnp.arange(n_rows, dtype=I32) % UNIT_ROWS
    row_dst = jnp.where(valid, row_slot, trash).astype(I32)
    return unit_e, unit_n, row_tok, row_dst


def _tile(n, want):
    want = min(n, want)
    assert n % want == 0, (n, want)
    return want


def _layer(h, p_i, ln_mix_g, w_in, pool_w, pool_scale, q_norm_g, k_norm_g, w_out, ln_ffn_g, router_w, router_b,
           w_gate_up, b_gate_up, w_down, b_down, ln_ple_g, ple_gate_w, ple_proj, ple_norm_g):
    b, s, d = h.shape
    t = b * s
    pool_width = d // 2
    attn_width = d // 2
    n_exp = router_w.shape[1]
    hidden = w_down.shape[1]
    assert pool_width % (LANES * len(POOL_WINDOWS)) == 0 and attn_width % LANES == 0
    x2 = h.reshape(t, d)

    proj = _in_proj(x2, ln_mix_g.reshape(1, d), w_in.astype(BF16), tm=_tile(t, 1024), tn=_tile(w_in.shape[1], 1024))
    proj3 = proj.reshape(b, s, -1)
    pool_out = _pool_mixer(proj3, pool_w.astype(BF16), pool_scale.reshape(1, pool_width), width=pool_width)
    attn_out = _stickbreak(proj3, q_norm_g, k_norm_g, pool_width=pool_width, attn_width=attn_width,
                           blk=_tile(s, 256))

    w_out_b = w_out.astype(BF16)
    h1, hn, top_e, gates, rank, counts = _mix_router(
        pool_out.reshape(t, pool_width), attn_out.reshape(t, attn_width), x2,
        w_out_b[:pool_width], w_out_b[pool_width:], ln_ffn_g.reshape(1, d),
        router_w.T.astype(BF16), router_b.reshape(n_exp, 1), tm=_tile(t, 512))

    n_units = t * TOP_K // UNIT_ROWS + n_exp + 2
    unit_e, unit_n, row_tok, row_dst = _routing_tables(top_e, rank, counts[:, 0], n_units=n_units, n_tok=t)
    wg, wu = _split_gate_up(w_gate_up, td=_tile(d, 256))
    bg = b_gate_up[:, 0::2].reshape(n_exp, 1, hidden)
    bu = b_gate_up[:, 1::2].reshape(n_exp, 1, hidden)
    slots = _moe_experts(hn, unit_e, unit_n, row_tok, row_dst, wg, wu, w_down, bg, bu,
                         b_down.reshape(n_exp, 1, d), n_slots=t * TOP_K + UNIT_ROWS, tf=_tile(hidden, 512))

    out = _ple_out(h1, slots, gates.T, p_i.reshape(t, -1),
                   ln_ple_g.reshape(1, d), ple_gate_w.astype(BF16), ple_proj.astype(BF16),
                   ple_norm_g.reshape(1, d), tm=_tile(t, 256))
    return out.reshape(b, s, d)


def kernel(x, p, ln_mix_g, w_in, pool_w, pool_scale, q_norm_g, k_norm_g, w_out, ln_ffn_g, router_w, router_b,
           w_gate_up, b_gate_up, w_down, b_down, ln_ple_g, ple_gate_w, ple_proj, ple_norm_g):
    h = x
    for i in range(p.shape[0]):
        h = _layer(h, p[i], ln_mix_g[i], w_in[i], pool_w[i], pool_scale[i], q_norm_g[i], k_norm_g[i], w_out[i],
                   ln_ffn_g[i], router_w[i], router_b[i], w_gate_up[i], b_gate_up[i], w_down[i], b_down[i],
                   ln_ple_g[i], ple_gate_w[i], ple_proj[i], ple_norm_g[i])
    return h
```

```python
import functools

import jax
import jax.numpy as jnp
from jax import lax
from jax.experimental import pallas as pl
from jax.experimental.pallas import tpu as pltpu

F32 = jnp.float32
BF16 = jnp.bfloat16
I32 = jnp.int32
U32 = jnp.uint32

HEAD_DIM = 64
POOL_WINDOWS = (2, 4, 8, 16)
TOP_K = 4
SWIGLU_LIMIT = 7.0
SWIGLU_ALPHA = 1.702
EPS = 1e-6
LOG2E = 1.4426950408889634

LANES = 128
VMEM_LIMIT = 56 * 1024 * 1024
UNIT_ROWS = 1024
SUB_ROWS = 256


def _params(*sem):
    return pltpu.CompilerParams(dimension_semantics=sem, vmem_limit_bytes=VMEM_LIMIT)


def _rms(x, g):
    return x * lax.rsqrt(jnp.mean(x * x, axis=-1, keepdims=True) + EPS) * g


def _pack_halves(x):
    n = x.shape[1] // 2
    lo = lax.bitcast_convert_type(x[:, :n].astype(BF16).astype(F32), U32)
    hi = lax.bitcast_convert_type(x[:, n:].astype(BF16).astype(F32), U32)
    return hi | (lo >> 16)


def _unpack_halves(w):
    lo = lax.bitcast_convert_type(w << 16, F32)
    hi = lax.bitcast_convert_type(w & jnp.uint32(0xFFFF0000), F32)
    return lo, hi


def _store_row_tiles(ref, first, packed):
    n, width = packed.shape
    m = width // LANES
    for c in range(m):
        ref[pl.ds(first * m + c, n, stride=m), :] = packed[:, c * LANES:(c + 1) * LANES]


def _load_row_tiles(ref, first, n, m):
    return jnp.concatenate([ref[pl.ds(first * m + c, n, stride=m), :] for c in range(m)], axis=1)


def _split_bf16(x):
    hi = x.astype(BF16)
    lo = (x - hi.astype(F32)).astype(BF16)
    return hi, lo


def _in_proj_kernel(x_ref, g_ref, w_ref, o_ref, xn_ref):
    @pl.when(pl.program_id(1) == 0)
    def _():
        xn_ref[...] = _rms(x_ref[...], g_ref[...]).astype(BF16)

    o_ref[...] = jnp.dot(xn_ref[...], w_ref[...], preferred_element_type=F32)


def _in_proj(x2, g, w, *, tm, tn):
    t, d = x2.shape
    n = w.shape[1]
    return pl.pallas_call(
        _in_proj_kernel,
        grid=(t // tm, n // tn),
        in_specs=[
            pl.BlockSpec((tm, d), lambda i, j: (i, 0)),
            pl.BlockSpec((1, d), lambda i, j: (0, 0)),
            pl.BlockSpec((d, tn), lambda i, j: (0, j)),
        ],
        out_specs=pl.BlockSpec((tm, tn), lambda i, j: (i, j)),
        out_shape=jax.ShapeDtypeStruct((t, n), F32),
        scratch_shapes=[pltpu.VMEM((tm, d), BF16)],
        compiler_params=_params("parallel", "arbitrary"),
        name="in_proj",
    )(x2, g, w)


def _pool_kernel(u_ref, w_ref, s_ref, o_ref, *, group):
    seq = u_ref.shape[1]
    row = lax.broadcasted_iota(I32, (seq, 1), 0)
    for gi, win in enumerate(POOL_WINDOWS):
        sl = slice(gi * group, (gi + 1) * group)
        u = u_ref[0, :, sl]
        acc = u
        span = 1
        while span < win:
            shifted = pltpu.roll(acc, span, axis=0)
            acc = acc + jnp.where(row >= span, shifted, 0.0)
            span *= 2
        count = jnp.minimum(row + 1, win).astype(F32)
        d = (acc / count - u).astype(BF16)
        y = jnp.dot(d, w_ref[gi], preferred_element_type=F32)
        o_ref[0, :, sl] = (y * s_ref[:, sl]).astype(BF16)


def _pool_mixer(proj3, pool_w, pool_scale, *, width):
    b, s, _ = proj3.shape
    group = width // len(POOL_WINDOWS)
    return pl.pallas_call(
        functools.partial(_pool_kernel, group=group),
        grid=(b,),
        in_specs=[
            pl.BlockSpec((1, s, width), lambda i: (i, 0, 0)),
            pl.BlockSpec((len(POOL_WINDOWS), group, group), lambda i: (0, 0, 0)),
            pl.BlockSpec((1, width), lambda i: (0, 0)),
        ],
        out_specs=pl.BlockSpec((1, s, width), lambda i: (i, 0, 0)),
        out_shape=jax.ShapeDtypeStruct((b, s, width), BF16),
        compiler_params=_params("parallel"),
        name="pool_mixer",
    )(proj3, pool_w, pool_scale)


def _head_rmsnorm(x, g2, bd):
    hi, lo = _split_bf16(x * x)
    ss = jnp.dot(hi, bd, preferred_element_type=F32) + jnp.dot(lo, bd, preferred_element_type=F32)
    return x * lax.rsqrt(ss * (1.0 / HEAD_DIM) + EPS) * g2


def _stickbreak_kernel(q_ref, k_ref, v_ref, qg_ref, kg_ref, bd_ref, tri_ref, o_ref,
                       kn_ref, vb_ref, q2_ref, carry_ref, acc_ref, *, blk):
    qi = pl.program_id(2)
    lane = lax.broadcasted_iota(I32, (1, LANES), 1)
    head0 = lane < HEAD_DIM

    @pl.when(qi == 0)
    def _():
        kn_ref[...] = _head_rmsnorm(k_ref[0], kg_ref[...], bd_ref[...]).astype(BF16)
        vb_ref[...] = v_ref[0].astype(BF16)

    q = _head_rmsnorm(q_ref[0], qg_ref[...], bd_ref[...]) * (HEAD_DIM ** -0.5 * LOG2E)
    q2_ref[...] = jnp.concatenate([jnp.where(head0, q, 0.0), jnp.where(head0, 0.0, q)], axis=0).astype(BF16)
    carry_ref[...] = jnp.zeros_like(carry_ref)
    acc_ref[...] = jnp.zeros_like(acc_ref)

    heads = (slice(0, blk), slice(blk, 2 * blk))

    def tiles(blocks, diagonal_first):
        starts = [pl.multiple_of(j * blk, blk) for j in blocks]
        chains = [(ks, rows) for ks in starts for rows in heads]
        zs = [lax.dot_general(q2_ref[rows, :], kn_ref[pl.ds(ks, blk), :], (((1,), (1,)), ((), ())),
                              preferred_element_type=F32) for ks, rows in chains]
        csums = []
        for i, z in enumerate(zs):
            sp = jnp.maximum(z, 0.0) + jnp.log2(1.0 + jnp.exp2(jnp.minimum(z, -z)))
            if diagonal_first and i < len(heads):
                r = lax.broadcasted_iota(I32, (blk, blk), 0)
                c = lax.broadcasted_iota(I32, (blk, blk), 1)
                keep = c < r
                sp = jnp.where(keep, sp, 0.0)
                zs[i] = jnp.where(keep, z, -1e30)
            csums.append(jnp.dot(sp.astype(BF16), tri_ref[:blk, :], preferred_element_type=F32))
        for (ks, rows), z, csum in zip(chains, zs, csums):
            carry = carry_ref[rows, :]
            a = jnp.concatenate(
                [jnp.exp2(z[:, h * LANES:(h + 1) * LANES] - csum[:, h * LANES:(h + 1) * LANES] - carry)
                 for h in range(blk // LANES)], axis=1)
            acc_ref[rows, :] += jnp.dot(a.astype(BF16), vb_ref[pl.ds(ks, blk), :], preferred_element_type=F32)
            carry_ref[rows, :] = carry + jnp.broadcast_to(csum[:, 0:1], carry.shape)

    odd = qi % 2

    @pl.when(odd == 0)
    def _():
        tiles([qi], True)

    @pl.when(odd == 1)
    def _():
        tiles([qi, qi - 1], True)

    def body(step, c):
        j = qi - 1 - odd - 2 * step
        tiles([j, j - 1], False)
        return c

    lax.fori_loop(0, (qi - odd) // 2, body, 0)
    o_ref[0] = jnp.where(head0, acc_ref[:blk], acc_ref[blk:]).astype(BF16)


def _stickbreak(proj3, q_norm_g, k_norm_g, *, pool_width, attn_width, blk):
    b, s, _ = proj3.shape
    pairs = attn_width // LANES
    qoff = pool_width // LANES
    koff = qoff + pairs
    voff = koff + pairs
    qg2 = jnp.concatenate([q_norm_g, q_norm_g]).reshape(1, LANES).astype(F32)
    kg2 = jnp.concatenate([k_norm_g, k_norm_g]).reshape(1, LANES).astype(F32)
    li = jnp.arange(LANES) // HEAD_DIM
    bd = (li[:, None] == li[None, :]).astype(BF16)
    tr = (jnp.arange(blk)[:, None] >= jnp.arange(blk)[None, :]).astype(BF16)
    tri = jnp.concatenate([tr, tr], axis=0)
    return pl.pallas_call(
        functools.partial(_stickbreak_kernel, blk=blk),
        grid=(b, pairs, s // blk),
        in_specs=[
            pl.BlockSpec((1, blk, LANES), lambda i, h, j: (i, j, qoff + h)),
            pl.BlockSpec((1, s, LANES), lambda i, h, j: (i, 0, koff + h)),
            pl.BlockSpec((1, s, LANES), lambda i, h, j: (i, 0, voff + h)),
            pl.BlockSpec((1, LANES), lambda i, h, j: (0, 0)),
            pl.BlockSpec((1, LANES), lambda i, h, j: (0, 0)),
            pl.BlockSpec((LANES, LANES), lambda i, h, j: (0, 0)),
            pl.BlockSpec((2 * blk, blk), lambda i, h, j: (0, 0)),
        ],
        out_specs=pl.BlockSpec((1, blk, LANES), lambda i, h, j: (i, j, h)),
        out_shape=jax.ShapeDtypeStruct((b, s, attn_width), BF16),
        scratch_shapes=[
            pltpu.VMEM((s, LANES), BF16),
            pltpu.VMEM((s, LANES), BF16),
            pltpu.VMEM((2 * blk, LANES), BF16),
            pltpu.VMEM((2 * blk, LANES), F32),
            pltpu.VMEM((2 * blk, LANES), F32),
        ],
        compiler_params=_params("parallel", "parallel", "arbitrary"),
        name="stickbreak",
    )(proj3, proj3, proj3, qg2, kg2, bd, tri)


def _mix_router_kernel(pool_ref, attn_ref, x_ref, wp_ref, wa_ref, g_ref, rw_ref, rb_ref, lt_ref,
                       h_ref, hn_ref, te_ref, tg_ref, rk_ref, cnt_ref, run_ref, *, n_exp):
    i = pl.program_id(0)

    @pl.when(i == 0)
    def _():
        run_ref[...] = jnp.zeros_like(run_ref)

    h = (x_ref[...]
         + jnp.dot(pool_ref[...], wp_ref[...], preferred_element_type=F32)
         + jnp.dot(attn_ref[...], wa_ref[...], preferred_element_type=F32))
    h_ref[...] = h
    hn = _rms(h, g_ref[...])
    _store_row_tiles(hn_ref, 0, _pack_halves(hn))
    logits = lax.dot_general(rw_ref[...], hn.astype(BF16), (((1,), (1,)), ((), ())),
                             preferred_element_type=F32) + rb_ref[...]
    tm = logits.shape[1]
    eidx = lax.broadcasted_iota(I32, (n_exp, tm), 0).astype(F32)
    vals, idxs, hots = [], [], []
    rem = logits
    for _ in range(TOP_K):
        m = jnp.max(rem, axis=0, keepdims=True)
        first = jnp.min(jnp.where(rem == m, eidx, float(n_exp)), axis=0, keepdims=True)
        hot = eidx == first
        vals.append(m)
        idxs.append(first)
        hots.append(hot)
        rem = jnp.where(hot, -jnp.inf, rem)
    ex = [jnp.exp(v - vals[0]) for v in vals]
    den = ex[0] + ex[1] + ex[2] + ex[3]
    tg_ref[...] = jnp.concatenate([e / den for e in ex], axis=0)
    te_ref[...] = jnp.concatenate(idxs, axis=0).astype(I32)
    hot_all = (hots[0] | hots[1] | hots[2] | hots[3]).astype(F32)
    before = jnp.dot(hot_all.astype(BF16), lt_ref[...], preferred_element_type=F32)
    base = run_ref[:, 0:1] + before
    rk_ref[...] = jnp.concatenate(
        [jnp.sum(jnp.where(hot, base, 0.0), axis=0, keepdims=True) for hot in hots], axis=0).astype(I32)
    run_ref[...] = run_ref[...] + jnp.sum(hot_all, axis=1, keepdims=True)
    cnt_ref[...] = run_ref[...].astype(I32)


def _mix_router(pool2, attn2, x2, w_pool, w_attn, g, rw_t, rb, *, tm):
    t, d = x2.shape
    half = pool2.shape[1]
    n_exp = rw_t.shape[0]
    lt = (jnp.arange(tm)[:, None] < jnp.arange(tm)[None, :]).astype(BF16)
    row = lambda i: (i, 0)
    fix = lambda i: (0, 0)
    col = lambda i: (0, i)
    return pl.pallas_call(
        functools.partial(_mix_router_kernel, n_exp=n_exp),
        grid=(t // tm,),
        in_specs=[
            pl.BlockSpec((tm, half), row),
            pl.BlockSpec((tm, half), row),
            pl.BlockSpec((tm, d), row),
            pl.BlockSpec((half, d), fix),
            pl.BlockSpec((half, d), fix),
            pl.BlockSpec((1, d), fix),
            pl.BlockSpec((n_exp, d), fix),
            pl.BlockSpec((n_exp, 1), fix),
            pl.BlockSpec((tm, tm), fix),
        ],
        out_specs=[
            pl.BlockSpec((tm, d), row),
            pl.BlockSpec((tm * (d // 2 // LANES), LANES), row),
            pl.BlockSpec((TOP_K, tm), col),
            pl.BlockSpec((TOP_K, tm), col),
            pl.BlockSpec((TOP_K, tm), col),
            pl.BlockSpec((n_exp, LANES), fix),
        ],
        out_shape=[
            jax.ShapeDtypeStruct((t, d), F32),
            jax.ShapeDtypeStruct((t * (d // 2 // LANES), LANES), U32),
            jax.ShapeDtypeStruct((TOP_K, t), I32),
            jax.ShapeDtypeStruct((TOP_K, t), F32),
            jax.ShapeDtypeStruct((TOP_K, t), I32),
            jax.ShapeDtypeStruct((n_exp, LANES), I32),
        ],
        scratch_shapes=[pltpu.VMEM((n_exp, LANES), F32)],
        compiler_params=_params("arbitrary"),
        name="mix_router",
    )(pool2, attn2, x2, w_pool, w_attn, g, rw_t, rb, lt)


def _split_gate_up_kernel(w_ref, p_ref, g_ref, u_ref):
    for c in range(w_ref.shape[2] // (2 * LANES)):
        w = w_ref[0, :, c * 2 * LANES:(c + 1) * 2 * LANES].astype(BF16)
        y = jnp.dot(w, p_ref[...], preferred_element_type=F32)
        g_ref[0, :, c * LANES:(c + 1) * LANES] = y[:, :LANES].astype(BF16)
        u_ref[0, :, c * LANES:(c + 1) * LANES] = y[:, LANES:].astype(BF16)


def _split_gate_up(w_gate_up, *, td):
    n_exp, d, two_f = w_gate_up.shape
    hidden = two_f // 2
    src = jnp.arange(2 * LANES)[:, None]
    col = jnp.arange(2 * LANES)[None, :]
    pick = (src == jnp.where(col < LANES, 2 * col, 2 * (col - LANES) + 1)).astype(BF16)
    out = jax.ShapeDtypeStruct((n_exp, d, hidden), BF16)
    return pl.pallas_call(
        _split_gate_up_kernel,
        grid=(n_exp, d // td),
        in_specs=[
            pl.BlockSpec((1, td, two_f), lambda e, i: (e, i, 0)),
            pl.BlockSpec((2 * LANES, 2 * LANES), lambda e, i: (0, 0)),
        ],
        out_specs=[pl.BlockSpec((1, td, hidden), lambda e, i: (e, i, 0))] * 2,
        out_shape=[out, out],
        compiler_params=_params("parallel", "parallel"),
        name="split_gate_up",
    )(w_gate_up, pick)


def _moe_kernel(ue_ref, un_ref, tokn_ref, dstp_ref, hn_hbm, wg_ref, wu_ref, wd_ref, bg_ref, bu_ref, bd_ref,
                out_hbm, xf_ref, xb_ref, acc_ref, stage_ref, gsem, ssem, *, chunk, trash_row):
    u = pl.program_id(0)
    f = pl.program_id(1)
    nu = pl.num_programs(0)
    nf = pl.num_programs(1)
    subs = UNIT_ROWS // SUB_ROWS
    half = xb_ref.shape[1] // 2
    nsub = (un_ref[u] + (SUB_ROWS - 1)) // SUB_ROWS
    live = un_ref[u] > 0
    prev_live = jnp.logical_and(u > 0, un_ref[jnp.maximum(u - 1, 0)] > 0)
    next_live = jnp.logical_and(u + 1 < nu, un_ref[jnp.minimum(u + 1, nu - 1)] > 0)
    gathered = jnp.logical_or(live, prev_live)
    gathers_next = jnp.logical_or(live, next_live)
    scatters_prev = jnp.logical_or(live, prev_live)

    m = half // LANES

    def whole_gather():
        return pltpu.make_async_copy(hn_hbm.at[pl.ds(0, UNIT_ROWS * m)], xf_ref, gsem)

    def whole_scatter():
        return pltpu.make_async_copy(stage_ref, out_hbm.at[pl.ds(0, UNIT_ROWS * m)], ssem)

    @pl.when(jnp.logical_and(u == 0, f == 0))
    def _():
        acc_ref[...] = jnp.zeros_like(acc_ref)
        stage_ref[...] = jnp.zeros_like(stage_ref)

    @pl.when(jnp.logical_and(jnp.logical_and(u > 0, f == 0), gathered))
    def _():
        whole_gather().wait()
        for s in range(subs):
            rows = slice(s * SUB_ROWS, (s + 1) * SUB_ROWS)
            lo, hi = _unpack_halves(_load_row_tiles(xf_ref, s * SUB_ROWS, SUB_ROWS, m))
            xb_ref[rows, :half] = lo.astype(BF16)
            xb_ref[rows, half:] = hi.astype(BF16)

    def gather_chunk(c):
        base = pl.multiple_of(c * chunk, chunk)
        for j in range(chunk):
            r = base + j
            src = pl.multiple_of(tokn_ref[r] * m, m)
            pltpu.make_async_copy(hn_hbm.at[pl.ds(src, m)], xf_ref.at[pl.ds(r * m, m)], gsem).start()

    def scatter_chunk(c):
        base = pl.multiple_of(c * chunk, chunk)
        for j in range(chunk):
            r = base + j
            dst = pl.multiple_of(dstp_ref[r] * m, m)
            pltpu.make_async_copy(stage_ref.at[pl.ds(r * m, m)], out_hbm.at[pl.ds(dst, m)], ssem).start()

    def expert_mlp(sub_blocks):
        for s in sub_blocks:
            gather_chunk(f * subs + s)
            scatter_chunk(f * subs + s)
        starts = [pl.multiple_of(s * SUB_ROWS, SUB_ROWS) for s in sub_blocks]
        pre = []
        for r0 in starts:
            xs = xb_ref[pl.ds(r0, SUB_ROWS), :]
            pre.append((jnp.dot(xs, wg_ref[0], preferred_element_type=F32),
                        jnp.dot(xs, wu_ref[0], preferred_element_type=F32)))
        wd = wd_ref[0].astype(BF16)
        for r0, (gate, up) in zip(starts, pre):
            gate = jnp.minimum(gate + bg_ref[0], SWIGLU_LIMIT)
            up = jnp.clip(up + bu_ref[0], -SWIGLU_LIMIT, SWIGLU_LIMIT)
            act = gate * jax.nn.sigmoid(SWIGLU_ALPHA * gate) * (up + 1.0)
            y = jnp.dot(act.astype(BF16), wd, preferred_element_type=F32)
            acc_ref[pl.ds(r0, SUB_ROWS), :] = y + jnp.where(f > 0, acc_ref[pl.ds(r0, SUB_ROWS), :], 0.0)

    def pair(i, c):
        expert_mlp([2 * i, 2 * i + 1])
        return c

    lax.fori_loop(0, nsub // 2, pair, 0)

    @pl.when(nsub % 2 == 1)
    def _():
        expert_mlp([nsub - 1])

    def rest(s, c):
        @pl.when(gathers_next)
        def _():
            gather_chunk(f * subs + s)

        @pl.when(scatters_prev)
        def _():
            scatter_chunk(f * subs + s)

        return c

    lax.fori_loop(nsub, subs, rest, 0)

    @pl.when(jnp.logical_and(f == nf - 1, scatters_prev))
    def _():
        whole_scatter().wait()

    @pl.when(jnp.logical_and(f == nf - 1, live))
    def _():
        for s in range(subs):
            rows = slice(s * SUB_ROWS, (s + 1) * SUB_ROWS)
            _store_row_tiles(stage_ref, s * SUB_ROWS, _pack_halves(acc_ref[rows, :] + bd_ref[0]))

    @pl.when(jnp.logical_and(u == nu - 1, f == nf - 1))
    def _():
        stage_ref[...] = jnp.zeros_like(stage_ref)
        fill = pltpu.make_async_copy(stage_ref, out_hbm.at[pl.ds(trash_row * m, UNIT_ROWS * m)], ssem)
        fill.start()
        fill.wait()


def _moe_experts(hn, unit_e, unit_n, row_tok, row_dst, wg, wu, wd, bg, bu, bd, *, n_slots, tf):
    d = wd.shape[2]
    m = d // 2 // LANES
    assert hn.shape[1] == LANES and hn.dtype == U32
    n_units = unit_e.shape[0]
    hidden = wg.shape[2]
    nf = hidden // tf
    subs = UNIT_ROWS // SUB_ROWS
    assert UNIT_ROWS % (nf * subs) == 0
    nxt = lambda u, f, ue, un: (jnp.minimum(u + 1, n_units - 1),)
    prv = lambda u, f, ue, un: (jnp.maximum(u - 1, 0),)
    grid_spec = pltpu.PrefetchScalarGridSpec(
        num_scalar_prefetch=2,
        grid=(n_units, nf),
        in_specs=[
            pl.BlockSpec((UNIT_ROWS,), nxt, memory_space=pltpu.SMEM),
            pl.BlockSpec((UNIT_ROWS,), prv, memory_space=pltpu.SMEM),
            pl.BlockSpec(memory_space=pl.ANY),
            pl.BlockSpec((1, d, tf), lambda u, f, ue, un: (ue[u], 0, f)),
            pl.BlockSpec((1, d, tf), lambda u, f, ue, un: (ue[u], 0, f)),
            pl.BlockSpec((1, tf, d), lambda u, f, ue, un: (ue[u], f, 0)),
            pl.BlockSpec((1, 1, tf), lambda u, f, ue, un: (ue[u], 0, f)),
            pl.BlockSpec((1, 1, tf), lambda u, f, ue, un: (ue[u], 0, f)),
            pl.BlockSpec((1, 1, d), lambda u, f, ue, un: (ue[u], 0, 0)),
        ],
        out_specs=pl.BlockSpec(memory_space=pl.ANY),
        scratch_shapes=[
            pltpu.VMEM((UNIT_ROWS * m, LANES), U32),
            pltpu.VMEM((UNIT_ROWS, d), BF16),
            pltpu.VMEM((UNIT_ROWS, d), F32),
            pltpu.VMEM((UNIT_ROWS * m, LANES), U32),
            pltpu.SemaphoreType.DMA(()),
            pltpu.SemaphoreType.DMA(()),
        ],
    )
    return pl.pallas_call(
        functools.partial(_moe_kernel, chunk=UNIT_ROWS // (nf * subs), trash_row=n_slots - UNIT_ROWS),
        grid_spec=grid_spec,
        out_shape=jax.ShapeDtypeStruct((n_slots * m, LANES), U32),
        compiler_params=_params("arbitrary", "arbitrary"),
        name="moe_experts",
    )(unit_e, unit_n, row_tok, row_dst, hn, wg, wu, wd, bg, bu, bd)


def _ple_kernel(h_ref, y0_ref, y1_ref, y2_ref, y3_ref, tg_ref, p_ref, g1_ref, wg_ref, wp_ref, g2_ref, o_ref):
    tm, d = h_ref.shape
    lo = h_ref[:, :d // 2]
    hi = h_ref[:, d // 2:]
    for k, y_ref in enumerate((y0_ref, y1_ref, y2_ref, y3_ref)):
        y_lo, y_hi = _unpack_halves(_load_row_tiles(y_ref, 0, tm, d // 2 // LANES))
        lo = lo + y_lo * tg_ref[:, k:k + 1]
        hi = hi + y_hi * tg_ref[:, k:k + 1]
    h = jnp.concatenate([lo, hi], axis=1)
    gate = jax.nn.sigmoid(jnp.dot(_rms(h, g1_ref[...]).astype(BF16), wg_ref[...], preferred_element_type=F32))
    e = _rms(jnp.dot(p_ref[...].astype(BF16), wp_ref[...], preferred_element_type=F32), g2_ref[...])
    o_ref[...] = h + gate * e


def _ple_out(h1, slots, gates_t, p2, g1, w_gate, w_proj, g2, *, tm):
    t, d = h1.shape
    pd = p2.shape[1]
    assert TOP_K == 4
    row = lambda i: (i, 0)
    fix = lambda i: (0, 0)
    m = d // 2 // LANES
    plane = lambda k: pl.BlockSpec((tm * m, LANES), lambda i: (k * (t // tm) + i, 0))
    return pl.pallas_call(
        _ple_kernel,
        grid=(t // tm,),
        in_specs=[
            pl.BlockSpec((tm, d), row),
            plane(0), plane(1), plane(2), plane(3),
            pl.BlockSpec((tm, TOP_K), row),
            pl.BlockSpec((tm, pd), row),
            pl.BlockSpec((1, d), fix),
            pl.BlockSpec((d, d), fix),
            pl.BlockSpec((pd, d), fix),
            pl.BlockSpec((1, d), fix),
        ],
        out_specs=pl.BlockSpec((tm, d), row),
        out_shape=jax.ShapeDtypeStruct((t, d), F32),
        compiler_params=_params("parallel"),
        name="ple_out",
    )(h1, slots, slots, slots, slots, gates_t, p2, g1, w_gate, w_proj, g2)


def _routing_tables(top_e, rank, counts, *, n_units, n_tok):
    n_exp = counts.shape[0]
    experts = jnp.arange(n_exp, dtype=I32)
    units_per = (counts + (UNIT_ROWS - 1)) // UNIT_ROWS
    unit_end = 1 + jnp.cumsum(units_per)
    unit_start = unit_end - units_per
    uid = jnp.arange(n_units, dtype=I32)
    unit_e = jnp.minimum(jnp.sum(unit_end[None, :] <= uid[:, None], axis=1), n_exp - 1).astype(I32)
    mine = unit_e[:, None] == experts[None, :]
    first = jnp.sum(jnp.where(mine, unit_start[None, :], 0), axis=1)
    total = jnp.sum(jnp.where(mine, counts[None, :], 0), axis=1)
    unit_n = jnp.where(uid >= 1, jnp.clip(total - (uid - first) * UNIT_ROWS, 0, UNIT_ROWS), 0).astype(I32)
    start = jnp.sum(jnp.where(top_e[:, :, None] == experts, unit_start, 0), axis=-1)
    row = (start + rank // UNIT_ROWS) * UNIT_ROWS + rank % UNIT_ROWS
    slot = jnp.arange(TOP_K, dtype=I32)[:, None] * n_tok + jnp.arange(n_tok, dtype=I32)[None, :]
    n_rows = n_units * UNIT_ROWS
    row_slot = jnp.full((n_rows,), -1, I32).at[row.reshape(-1)].set(
        slot.reshape(-1), unique_indices=True, mode="promise_in_bounds")
    valid = row_slot >= 0
    row_tok = jnp.where(valid, row_slot % n_tok, 0).astype(I32)
    trash = TOP_K * n_tok + jnp.arange(n_rows, dtype=I32) % UNIT_ROWS
    row_dst = jnp.where(valid, row_slot, trash).astype(I32)
    return unit_e, unit_n, row_tok, row_dst


def _tile(n, want):
    want = min(n, want)
    assert n % want == 0, (n, want)
    return want


def _layer(h, p_i, ln_mix_g, w_in, pool_w, pool_scale, q_norm_g, k_norm_g, w_out, ln_ffn_g, router_w, router_b,
           w_gate_up, b_gate_up, w_down, b_down, ln_ple_g, ple_gate_w, ple_proj, ple_norm_g):
    b, s, d = h.shape
    t = b * s
    pool_width = d // 2
    attn_width = d // 2
    n_exp = router_w.shape[1]
    hidden = w_down.shape[1]
    assert pool_width % (LANES * len(POOL_WINDOWS)) == 0 and attn_width % LANES == 0
    x2 = h.reshape(t, d)

    proj = _in_proj(x2, ln_mix_g.reshape(1, d), w_in.astype(BF16), tm=_tile(t, 1024), tn=_tile(w_in.shape[1], 1024))
    proj3 = proj.reshape(b, s, -1)
    pool_out = _pool_mixer(proj3, pool_w.astype(BF16), pool_scale.reshape(1, pool_width), width=pool_width)
    attn_out = _stickbreak(proj3, q_norm_g, k_norm_g, pool_width=pool_width, attn_width=attn_width,
                           blk=_tile(s, 256))

    w_out_b = w_out.astype(BF16)
    h1, hn, top_e, gates, rank, counts = _mix_router(
        pool_out.reshape(t, pool_width), attn_out.reshape(t, attn_width), x2,
        w_out_b[:pool_width], w_out_b[pool_width:], ln_ffn_g.reshape(1, d),
        router_w.T.astype(BF16), router_b.reshape(n_exp, 1), tm=_tile(t, 512))

    n_units = t * TOP_K // UNIT_ROWS + n_exp + 2
    unit_e, unit_n, row_tok, row_dst = _routing_tables(top_e, rank, counts[:, 0], n_units=n_units, n_tok=t)
    wg, wu = _split_gate_up(w_gate_up, td=_tile(d, 256))
    bg = b_gate_up[:, 0::2].reshape(n_exp, 1, hidden)
    bu = b_gate_up[:, 1::2].reshape(n_exp, 1, hidden)
    slots = _moe_experts(hn, unit_e, unit_n, row_tok, row_dst, wg, wu, w_down, bg, bu,
                         b_down.reshape(n_exp, 1, d), n_slots=t * TOP_K + UNIT_ROWS, tf=_tile(hidden, 512))

    out = _ple_out(h1, slots, gates.T, p_i.reshape(t, -1),
                   ln_ple_g.reshape(1, d), ple_gate_w.astype(BF16), ple_proj.astype(BF16),
                   ple_norm_g.reshape(1, d), tm=_tile(t, 256))
    return out.reshape(b, s, d)


def kernel(x, p, ln_mix_g, w_in, pool_w, pool_scale, q_norm_g, k_norm_g, w_out, ln_ffn_g, router_w, router_b,
           w_gate_up, b_gate_up, w_down, b_down, ln_ple_g, ple_gate_w, ple_proj, ple_norm_g):
    h = x
    for i in range(p.shape[0]):
        h = _layer(h, p[i], ln_mix_g[i], w_in[i], pool_w[i], pool_scale[i], q_norm_g[i], k_norm_g[i], w_out[i],
                   ln_ffn_g[i], router_w[i], router_b[i], w_gate_up[i], b_gate_up[i], w_down[i], b_down[i],
                   ln_ple_g[i], ple_gate_w[i], ple_proj[i], ple_norm_g[i])
    return h
```

```python
import functools

import jax
import jax.numpy as jnp
from jax import lax
from jax.experimental import pallas as pl
from jax.experimental.pallas import tpu as pltpu

F32 = jnp.float32
BF16 = jnp.bfloat16
I32 = jnp.int32
U32 = jnp.uint32

HEAD_DIM = 64
POOL_WINDOWS = (2, 4, 8, 16)
TOP_K = 4
SWIGLU_LIMIT = 7.0
SWIGLU_ALPHA = 1.702
EPS = 1e-6
LOG2E = 1.4426950408889634

LANES = 128
VMEM_LIMIT = 56 * 1024 * 1024
UNIT_ROWS = 1024
SUB_ROWS = 512


def _params(*sem):
    return pltpu.CompilerParams(dimension_semantics=sem, vmem_limit_bytes=VMEM_LIMIT)


def _rms(x, g):
    return x * lax.rsqrt(jnp.mean(x * x, axis=-1, keepdims=True) + EPS) * g


def _pack_halves(x):
    n = x.shape[1] // 2
    lo = lax.bitcast_convert_type(x[:, :n].astype(BF16).astype(F32), U32)
    hi = lax.bitcast_convert_type(x[:, n:].astype(BF16).astype(F32), U32)
    return hi | (lo >> 16)


def _unpack_halves(w):
    lo = lax.bitcast_convert_type(w << 16, F32)
    hi = lax.bitcast_convert_type(w & jnp.uint32(0xFFFF0000), F32)
    return lo, hi


def _store_row_tiles(ref, first, packed):
    n, width = packed.shape
    m = width // LANES
    for c in range(m):
        ref[pl.ds(first * m + c, n, stride=m), :] = packed[:, c * LANES:(c + 1) * LANES]


def _load_row_tiles(ref, first, n, m):
    return jnp.concatenate([ref[pl.ds(first * m + c, n, stride=m), :] for c in range(m)], axis=1)


def _split_bf16(x):
    hi = x.astype(BF16)
    lo = (x - hi.astype(F32)).astype(BF16)
    return hi, lo


def _in_proj_kernel(x_ref, g_ref, w_ref, o_ref, xn_ref):
    @pl.when(pl.program_id(1) == 0)
    def _():
        xn_ref[...] = _rms(x_ref[...], g_ref[...]).astype(BF16)

    o_ref[...] = jnp.dot(xn_ref[...], w_ref[...], preferred_element_type=F32)


def _in_proj(x2, g, w, *, tm, tn):
    t, d = x2.shape
    n = w.shape[1]
    return pl.pallas_call(
        _in_proj_kernel,
        grid=(t // tm, n // tn),
        in_specs=[
            pl.BlockSpec((tm, d), lambda i, j: (i, 0)),
            pl.BlockSpec((1, d), lambda i, j: (0, 0)),
            pl.BlockSpec((d, tn), lambda i, j: (0, j)),
        ],
        out_specs=pl.BlockSpec((tm, tn), lambda i, j: (i, j)),
        out_shape=jax.ShapeDtypeStruct((t, n), F32),
        scratch_shapes=[pltpu.VMEM((tm, d), BF16)],
        compiler_params=_params("parallel", "arbitrary"),
        name="in_proj",
    )(x2, g, w)


def _pool_kernel(u_ref, w_ref, s_ref, o_ref, *, group):
    seq = u_ref.shape[1]
    row = lax.broadcasted_iota(I32, (seq, 1), 0)
    for gi, win in enumerate(POOL_WINDOWS):
        sl = slice(gi * group, (gi + 1) * group)
        u = u_ref[0, :, sl]
        acc = u
        span = 1
        while span < win:
            shifted = pltpu.roll(acc, span, axis=0)
            acc = acc + jnp.where(row >= span, shifted, 0.0)
            span *= 2
        count = jnp.minimum(row + 1, win).astype(F32)
        d = (acc / count - u).astype(BF16)
        y = jnp.dot(d, w_ref[gi], preferred_element_type=F32)
        o_ref[0, :, sl] = (y * s_ref[:, sl]).astype(BF16)


def _pool_mixer(proj3, pool_w, pool_scale, *, width):
    b, s, _ = proj3.shape
    group = width // len(POOL_WINDOWS)
    return pl.pallas_call(
        functools.partial(_pool_kernel, group=group),
        grid=(b,),
        in_specs=[
            pl.BlockSpec((1, s, width), lambda i: (i, 0, 0)),
            pl.BlockSpec((len(POOL_WINDOWS), group, group), lambda i: (0, 0, 0)),
            pl.BlockSpec((1, width), lambda i: (0, 0)),
        ],
        out_specs=pl.BlockSpec((1, s, width), lambda i: (i, 0, 0)),
        out_shape=jax.ShapeDtypeStruct((b, s, width), BF16),
        compiler_params=_params("parallel"),
        name="pool_mixer",
    )(proj3, pool_w, pool_scale)


def _head_rmsnorm(x, g2, bd):
    hi, lo = _split_bf16(x * x)
    ss = jnp.dot(hi, bd, preferred_element_type=F32) + jnp.dot(lo, bd, preferred_element_type=F32)
    return x * lax.rsqrt(ss * (1.0 / HEAD_DIM) + EPS) * g2


def _stickbreak_kernel(q_ref, k_ref, v_ref, qg_ref, kg_ref, bd_ref, tri_ref, o_ref,
                       kn_ref, vb_ref, q2_ref, carry_ref, acc_ref, *, blk):
    qi = pl.program_id(2)
    lane = lax.broadcasted_iota(I32, (1, LANES), 1)
    head0 = lane < HEAD_DIM

    @pl.when(qi == 0)
    def _():
        kn_ref[...] = _head_rmsnorm(k_ref[0], kg_ref[...], bd_ref[...]).astype(BF16)
        vb_ref[...] = v_ref[0].astype(BF16)

    q = _head_rmsnorm(q_ref[0], qg_ref[...], bd_ref[...]) * (HEAD_DIM ** -0.5 * LOG2E)
    q2_ref[...] = jnp.concatenate([jnp.where(head0, q, 0.0), jnp.where(head0, 0.0, q)], axis=0).astype(BF16)
    carry_ref[...] = jnp.zeros_like(carry_ref)
    acc_ref[...] = jnp.zeros_like(acc_ref)

    heads = (slice(0, blk), slice(blk, 2 * blk))

    def tiles(blocks, diagonal_first):
        starts = [pl.multiple_of(j * blk, blk) for j in blocks]
        chains = [(ks, rows) for ks in starts for rows in heads]
        zs = [lax.dot_general(q2_ref[rows, :], kn_ref[pl.ds(ks, blk), :], (((1,), (1,)), ((), ())),
                              preferred_element_type=F32) for ks, rows in chains]
        csums = []
        for i, z in enumerate(zs):
            sp = jnp.maximum(z, 0.0) + jnp.log2(1.0 + jnp.exp2(jnp.minimum(z, -z)))
            if diagonal_first and i < len(heads):
                r = lax.broadcasted_iota(I32, (blk, blk), 0)
                c = lax.broadcasted_iota(I32, (blk, blk), 1)
                keep = c < r
                sp = jnp.where(keep, sp, 0.0)
                zs[i] = jnp.where(keep, z, -1e30)
            csums.append(jnp.dot(sp.astype(BF16), tri_ref[:blk, :], preferred_element_type=F32))
        for (ks, rows), z, csum in zip(chains, zs, csums):
            carry = carry_ref[rows, :]
            a = jnp.concatenate(
                [jnp.exp2(z[:, h * LANES:(h + 1) * LANES] - csum[:, h * LANES:(h + 1) * LANES] - carry)
                 for h in range(blk // LANES)], axis=1)
            acc_ref[rows, :] += jnp.dot(a.astype(BF16), vb_ref[pl.ds(ks, blk), :], preferred_element_type=F32)
            carry_ref[rows, :] = carry + jnp.broadcast_to(csum[:, 0:1], carry.shape)

    odd = qi % 2

    @pl.when(odd == 0)
    def _():
        tiles([qi], True)

    @pl.when(odd == 1)
    def _():
        tiles([qi, qi - 1], True)

    def body(step, c):
        j = qi - 1 - odd - 2 * step
        tiles([j, j - 1], False)
        return c

    lax.fori_loop(0, (qi - odd) // 2, body, 0)
    o_ref[0] = jnp.where(head0, acc_ref[:blk], acc_ref[blk:]).astype(BF16)


def _stickbreak(proj3, q_norm_g, k_norm_g, *, pool_width, attn_width, blk):
    b, s, _ = proj3.shape
    pairs = attn_width // LANES
    qoff = pool_width // LANES
    koff = qoff + pairs
    voff = koff + pairs
    qg2 = jnp.concatenate([q_norm_g, q_norm_g]).reshape(1, LANES).astype(F32)
    kg2 = jnp.concatenate([k_norm_g, k_norm_g]).reshape(1, LANES).astype(F32)
    li = jnp.arange(LANES) // HEAD_DIM
    bd = (li[:, None] == li[None, :]).astype(BF16)
    tr = (jnp.arange(blk)[:, None] >= jnp.arange(blk)[None, :]).astype(BF16)
    tri = jnp.concatenate([tr, tr], axis=0)
    return pl.pallas_call(
        functools.partial(_stickbreak_kernel, blk=blk),
        grid=(b, pairs, s // blk),
        in_specs=[
            pl.BlockSpec((1, blk, LANES), lambda i, h, j: (i, j, qoff + h)),
            pl.BlockSpec((1, s, LANES), lambda i, h, j: (i, 0, koff + h)),
            pl.BlockSpec((1, s, LANES), lambda i, h, j: (i, 0, voff + h)),
            pl.BlockSpec((1, LANES), lambda i, h, j: (0, 0)),
            pl.BlockSpec((1, LANES), lambda i, h, j: (0, 0)),
            pl.BlockSpec((LANES, LANES), lambda i, h, j: (0, 0)),
            pl.BlockSpec((2 * blk, blk), lambda i, h, j: (0, 0)),
        ],
        out_specs=pl.BlockSpec((1, blk, LANES), lambda i, h, j: (i, j, h)),
        out_shape=jax.ShapeDtypeStruct((b, s, attn_width), BF16),
        scratch_shapes=[
            pltpu.VMEM((s, LANES), BF16),
            pltpu.VMEM((s, LANES), BF16),
            pltpu.VMEM((2 * blk, LANES), BF16),
            pltpu.VMEM((2 * blk, LANES), F32),
            pltpu.VMEM((2 * blk, LANES), F32),
        ],
        compiler_params=_params("parallel", "parallel", "arbitrary"),
        name="stickbreak",
    )(proj3, proj3, proj3, qg2, kg2, bd, tri)


def _mix_router_kernel(pool_ref, attn_ref, x_ref, wp_ref, wa_ref, g_ref, rw_ref, rb_ref, lt_ref,
                       h_ref, hn_ref, te_ref, tg_ref, rk_ref, cnt_ref, run_ref, *, n_exp):
    i = pl.program_id(0)

    @pl.when(i == 0)
    def _():
        run_ref[...] = jnp.zeros_like(run_ref)

    h = (x_ref[...]
         + jnp.dot(pool_ref[...], wp_ref[...], preferred_element_type=F32)
         + jnp.dot(attn_ref[...], wa_ref[...], preferred_element_type=F32))
    h_ref[...] = h
    hn = _rms(h, g_ref[...])
    _store_row_tiles(hn_ref, 0, _pack_halves(hn))
    logits = lax.dot_general(rw_ref[...], hn.astype(BF16), (((1,), (1,)), ((), ())),
                             preferred_element_type=F32) + rb_ref[...]
    tm = logits.shape[1]
    eidx = lax.broadcasted_iota(I32, (n_exp, tm), 0).astype(F32)
    vals, idxs, hots = [], [], []
    rem = logits
    for _ in range(TOP_K):
        m = jnp.max(rem, axis=0, keepdims=True)
        first = jnp.min(jnp.where(rem == m, eidx, float(n_exp)), axis=0, keepdims=True)
        hot = eidx == first
        vals.append(m)
        idxs.append(first)
        hots.append(hot)
        rem = jnp.where(hot, -jnp.inf, rem)
    ex = [jnp.exp(v - vals[0]) for v in vals]
    den = ex[0] + ex[1] + ex[2] + ex[3]
    tg_ref[...] = jnp.concatenate([e / den for e in ex], axis=0)
    te_ref[...] = jnp.concatenate(idxs, axis=0).astype(I32)
    hot_all = (hots[0] | hots[1] | hots[2] | hots[3]).astype(F32)
    before = jnp.dot(hot_all.astype(BF16), lt_ref[...], preferred_element_type=F32)
    base = run_ref[:, 0:1] + before
    rk_ref[...] = jnp.concatenate(
        [jnp.sum(jnp.where(hot, base, 0.0), axis=0, keepdims=True) for hot in hots], axis=0).astype(I32)
    run_ref[...] = run_ref[...] + jnp.sum(hot_all, axis=1, keepdims=True)
    cnt_ref[...] = run_ref[...].astype(I32)


def _mix_router(pool2, attn2, x2, w_pool, w_attn, g, rw_t, rb, *, tm):
    t, d = x2.shape
    half = pool2.shape[1]
    n_exp = rw_t.shape[0]
    lt = (jnp.arange(tm)[:, None] < jnp.arange(tm)[None, :]).astype(BF16)
    row = lambda i: (i, 0)
    fix = lambda i: (0, 0)
    col = lambda i: (0, i)
    return pl.pallas_call(
        functools.partial(_mix_router_kernel, n_exp=n_exp),
        grid=(t // tm,),
        in_specs=[
            pl.BlockSpec((tm, half), row),
            pl.BlockSpec((tm, half), row),
            pl.BlockSpec((tm, d), row),
            pl.BlockSpec((half, d), fix),
            pl.BlockSpec((half, d), fix),
            pl.BlockSpec((1, d), fix),
            pl.BlockSpec((n_exp, d), fix),
            pl.BlockSpec((n_exp, 1), fix),
            pl.BlockSpec((tm, tm), fix),
        ],
        out_specs=[
            pl.BlockSpec((tm, d), row),
            pl.BlockSpec((tm * (d // 2 // LANES), LANES), row),
            pl.BlockSpec((TOP_K, tm), col),
            pl.BlockSpec((TOP_K, tm), col),
            pl.BlockSpec((TOP_K, tm), col),
            pl.BlockSpec((n_exp, LANES), fix),
        ],
        out_shape=[
            jax.ShapeDtypeStruct((t, d), F32),
            jax.ShapeDtypeStruct((t * (d // 2 // LANES), LANES), U32),
            jax.ShapeDtypeStruct((TOP_K, t), I32),
            jax.ShapeDtypeStruct((TOP_K, t), F32),
            jax.ShapeDtypeStruct((TOP_K, t), I32),
            jax.ShapeDtypeStruct((n_exp, LANES), I32),
        ],
        scratch_shapes=[pltpu.VMEM((n_exp, LANES), F32)],
        compiler_params=_params("arbitrary"),
        name="mix_router",
    )(pool2, attn2, x2, w_pool, w_attn, g, rw_t, rb, lt)


def _split_gate_up_kernel(w_ref, p_ref, g_ref, u_ref):
    for c in range(w_ref.shape[2] // (2 * LANES)):
        w = w_ref[0, :, c * 2 * LANES:(c + 1) * 2 * LANES].astype(BF16)
        y = jnp.dot(w, p_ref[...], preferred_element_type=F32)
        g_ref[0, :, c * LANES:(c + 1) * LANES] = y[:, :LANES].astype(BF16)
        u_ref[0, :, c * LANES:(c + 1) * LANES] = y[:, LANES:].astype(BF16)


def _split_gate_up(w_gate_up, *, td):
    n_exp, d, two_f = w_gate_up.shape
    hidden = two_f // 2
    src = jnp.arange(2 * LANES)[:, None]
    col = jnp.arange(2 * LANES)[None, :]
    pick = (src == jnp.where(col < LANES, 2 * col, 2 * (col - LANES) + 1)).astype(BF16)
    out = jax.ShapeDtypeStruct((n_exp, d, hidden), BF16)
    return pl.pallas_call(
        _split_gate_up_kernel,
        grid=(n_exp, d // td),
        in_specs=[
            pl.BlockSpec((1, td, two_f), lambda e, i: (e, i, 0)),
            pl.BlockSpec((2 * LANES, 2 * LANES), lambda e, i: (0, 0)),
        ],
        out_specs=[pl.BlockSpec((1, td, hidden), lambda e, i: (e, i, 0))] * 2,
        out_shape=[out, out],
        compiler_params=_params("parallel", "parallel"),
        name="split_gate_up",
    )(w_gate_up, pick)


def _moe_kernel(ue_ref, un_ref, tokn_ref, dstp_ref, hn_hbm, wg_ref, wu_ref, wd_ref, bg_ref, bu_ref, bd_ref,
                out_hbm, xf_ref, xb_ref, acc_ref, stage_ref, gsem, ssem, *, chunk, trash_row):
    u = pl.program_id(0)
    f = pl.program_id(1)
    nu = pl.num_programs(0)
    nf = pl.num_programs(1)
    subs = UNIT_ROWS // SUB_ROWS
    half = xb_ref.shape[1] // 2
    nsub = (un_ref[u] + (SUB_ROWS - 1)) // SUB_ROWS
    live = un_ref[u] > 0
    prev_live = jnp.logical_and(u > 0, un_ref[jnp.maximum(u - 1, 0)] > 0)
    next_live = jnp.logical_and(u + 1 < nu, un_ref[jnp.minimum(u + 1, nu - 1)] > 0)
    gathered = jnp.logical_or(live, prev_live)
    gathers_next = jnp.logical_or(live, next_live)
    scatters_prev = jnp.logical_or(live, prev_live)

    m = half // LANES

    def whole_gather():
        return pltpu.make_async_copy(hn_hbm.at[pl.ds(0, UNIT_ROWS * m)], xf_ref, gsem)

    def whole_scatter():
        return pltpu.make_async_copy(stage_ref, out_hbm.at[pl.ds(0, UNIT_ROWS * m)], ssem)

    @pl.when(jnp.logical_and(u == 0, f == 0))
    def _():
        acc_ref[...] = jnp.zeros_like(acc_ref)
        stage_ref[...] = jnp.zeros_like(stage_ref)

    @pl.when(jnp.logical_and(jnp.logical_and(u > 0, f == 0), gathered))
    def _():
        whole_gather().wait()
        for s in range(subs):
            rows = slice(s * SUB_ROWS, (s + 1) * SUB_ROWS)
            lo, hi = _unpack_halves(_load_row_tiles(xf_ref, s * SUB_ROWS, SUB_ROWS, m))
            xb_ref[rows, :half] = lo.astype(BF16)
            xb_ref[rows, half:] = hi.astype(BF16)

    def gather_chunk(c):
        base = pl.multiple_of(c * chunk, chunk)
        for j in range(chunk):
            r = base + j
            src = pl.multiple_of(tokn_ref[r] * m, m)
            pltpu.make_async_copy(hn_hbm.at[pl.ds(src, m)], xf_ref.at[pl.ds(r * m, m)], gsem).start()

    def scatter_chunk(c):
        base = pl.multiple_of(c * chunk, chunk)
        for j in range(chunk):
            r = base + j
            dst = pl.multiple_of(dstp_ref[r] * m, m)
            pltpu.make_async_copy(stage_ref.at[pl.ds(r * m, m)], out_hbm.at[pl.ds(dst, m)], ssem).start()

    def expert_mlp(sub_blocks):
        for s in sub_blocks:
            gather_chunk(f * subs + s)
            scatter_chunk(f * subs + s)
        starts = [pl.multiple_of(s * SUB_ROWS, SUB_ROWS) for s in sub_blocks]
        pre = []
        for r0 in starts:
            xs = xb_ref[pl.ds(r0, SUB_ROWS), :]
            pre.append((jnp.dot(xs, wg_ref[0], preferred_element_type=F32),
                        jnp.dot(xs, wu_ref[0], preferred_element_type=F32)))
        wd = wd_ref[0].astype(BF16)
        for r0, (gate, up) in zip(starts, pre):
            gate = jnp.minimum(gate + bg_ref[0], SWIGLU_LIMIT)
            up = jnp.clip(up + bu_ref[0], -SWIGLU_LIMIT, SWIGLU_LIMIT)
            act = gate * jax.nn.sigmoid(SWIGLU_ALPHA * gate) * (up + 1.0)
            y = jnp.dot(act.astype(BF16), wd, preferred_element_type=F32)
            acc_ref[pl.ds(r0, SUB_ROWS), :] = y + jnp.where(f > 0, acc_ref[pl.ds(r0, SUB_ROWS), :], 0.0)

    def pair(i, c):
        expert_mlp([2 * i, 2 * i + 1])
        return c

    lax.fori_loop(0, nsub // 2, pair, 0)

    @pl.when(nsub % 2 == 1)
    def _():
        expert_mlp([nsub - 1])

    def rest(s, c):
        @pl.when(gathers_next)
        def _():
            gather_chunk(f * subs + s)

        @pl.when(scatters_prev)
        def _():
            scatter_chunk(f * subs + s)

        return c

    lax.fori_loop(nsub, subs, rest, 0)

    @pl.when(jnp.logical_and(f == nf - 1, scatters_prev))
    def _():
        whole_scatter().wait()

    @pl.when(jnp.logical_and(f == nf - 1, live))
    def _():
        for s in range(subs):
            rows = slice(s * SUB_ROWS, (s + 1) * SUB_ROWS)
            _store_row_tiles(stage_ref, s * SUB_ROWS, _pack_halves(acc_ref[rows, :] + bd_ref[0]))

    @pl.when(jnp.logical_and(u == nu - 1, f == nf - 1))
    def _():
        stage_ref[...] = jnp.zeros_like(stage_ref)
        fill = pltpu.make_async_copy(stage_ref, out_hbm.at[pl.ds(trash_row * m, UNIT_ROWS * m)], ssem)
        fill.start()
        fill.wait()


def _moe_experts(hn, unit_e, unit_n, row_tok, row_dst, wg, wu, wd, bg, bu, bd, *, n_slots, tf):
    d = wd.shape[2]
    m = d // 2 // LANES
    assert hn.shape[1] == LANES and hn.dtype == U32
    n_units = unit_e.shape[0]
    hidden = wg.shape[2]
    nf = hidden // tf
    subs = UNIT_ROWS // SUB_ROWS
    assert UNIT_ROWS % (nf * subs) == 0
    nxt = lambda u, f, ue, un: (jnp.minimum(u + 1, n_units - 1),)
    prv = lambda u, f, ue, un: (jnp.maximum(u - 1, 0),)
    grid_spec = pltpu.PrefetchScalarGridSpec(
        num_scalar_prefetch=2,
        grid=(n_units, nf),
        in_specs=[
            pl.BlockSpec((UNIT_ROWS,), nxt, memory_space=pltpu.SMEM),
            pl.BlockSpec((UNIT_ROWS,), prv, memory_space=pltpu.SMEM),
            pl.BlockSpec(memory_space=pl.ANY),
            pl.BlockSpec((1, d, tf), lambda u, f, ue, un: (ue[u], 0, f)),
            pl.BlockSpec((1, d, tf), lambda u, f, ue, un: (ue[u], 0, f)),
            pl.BlockSpec((1, tf, d), lambda u, f, ue, un: (ue[u], f, 0)),
            pl.BlockSpec((1, 1, tf), lambda u, f, ue, un: (ue[u], 0, f)),
            pl.BlockSpec((1, 1, tf), lambda u, f, ue, un: (ue[u], 0, f)),
            pl.BlockSpec((1, 1, d), lambda u, f, ue, un: (ue[u], 0, 0)),
        ],
        out_specs=pl.BlockSpec(memory_space=pl.ANY),
        scratch_shapes=[
            pltpu.VMEM((UNIT_ROWS * m, LANES), U32),
            pltpu.VMEM((UNIT_ROWS, d), BF16),
            pltpu.VMEM((UNIT_ROWS, d), F32),
            pltpu.VMEM((UNIT_ROWS * m, LANES), U32),
            pltpu.SemaphoreType.DMA(()),
            pltpu.SemaphoreType.DMA(()),
        ],
    )
    return pl.pallas_call(
        functools.partial(_moe_kernel, chunk=UNIT_ROWS // (nf * subs), trash_row=n_slots - UNIT_ROWS),
        grid_spec=grid_spec,
        out_shape=jax.ShapeDtypeStruct((n_slots * m, LANES), U32),
        compiler_params=_params("arbitrary", "arbitrary"),
        name="moe_experts",
    )(unit_e, unit_n, row_tok, row_dst, hn, wg, wu, wd, bg, bu, bd)


def _ple_kernel(h_ref, y0_ref, y1_ref, y2_ref, y3_ref, tg_ref, p_ref, g1_ref, wg_ref, wp_ref, g2_ref, o_ref):
    tm, d = h_ref.shape
    lo = h_ref[:, :d // 2]
    hi = h_ref[:, d // 2:]
    for k, y_ref in enumerate((y0_ref, y1_ref, y2_ref, y3_ref)):
        y_lo, y_hi = _unpack_halves(_load_row_tiles(y_ref, 0, tm, d // 2 // LANES))
        lo = lo + y_lo * tg_ref[:, k:k + 1]
        hi = hi + y_hi * tg_ref[:, k:k + 1]
    h = jnp.concatenate([lo, hi], axis=1)
    gate = jax.nn.sigmoid(jnp.dot(_rms(h, g1_ref[...]).astype(BF16), wg_ref[...], preferred_element_type=F32))
    e = _rms(jnp.dot(p_ref[...].astype(BF16), wp_ref[...], preferred_element_type=F32), g2_ref[...])
    o_ref[...] = h + gate * e


def _ple_out(h1, slots, gates_t, p2, g1, w_gate, w_proj, g2, *, tm):
    t, d = h1.shape
    pd = p2.shape[1]
    assert TOP_K == 4
    row = lambda i: (i, 0)
    fix = lambda i: (0, 0)
    m = d // 2 // LANES
    plane = lambda k: pl.BlockSpec((tm * m, LANES), lambda i: (k * (t // tm) + i, 0))
    return pl.pallas_call(
        _ple_kernel,
        grid=(t // tm,),
        in_specs=[
            pl.BlockSpec((tm, d), row),
            plane(0), plane(1), plane(2), plane(3),
            pl.BlockSpec((tm, TOP_K), row),
            pl.BlockSpec((tm, pd), row),
            pl.BlockSpec((1, d), fix),
            pl.BlockSpec((d, d), fix),
            pl.BlockSpec((pd, d), fix),
            pl.BlockSpec((1, d), fix),
        ],
        out_specs=pl.BlockSpec((tm, d), row),
        out_shape=jax.ShapeDtypeStruct((t, d), F32),
        compiler_params=_params("parallel"),
        name="ple_out",
    )(h1, slots, slots, slots, slots, gates_t, p2, g1, w_gate, w_proj, g2)


def _routing_tables(top_e, rank, counts, *, n_units, n_tok):
    n_exp = counts.shape[0]
    experts = jnp.arange(n_exp, dtype=I32)
    units_per = (counts + (UNIT_ROWS - 1)) // UNIT_ROWS
    unit_end = 1 + jnp.cumsum(units_per)
    unit_start = unit_end - units_per
    uid = jnp.arange(n_units, dtype=I32)
    unit_e = jnp.minimum(jnp.sum(unit_end[None, :] <= uid[:, None], axis=1), n_exp - 1).astype(I32)
    mine = unit_e[:, None] == experts[None, :]
    first = jnp.sum(jnp.where(mine, unit_start[None, :], 0), axis=1)
    total = jnp.sum(jnp.where(mine, counts[None, :], 0), axis=1)
    unit_n = jnp.where(uid >= 1, jnp.clip(total - (uid - first) * UNIT_ROWS, 0, UNIT_ROWS), 0).astype(I32)
    start = jnp.sum(jnp.where(top_e[:, :, None] == experts, unit_start, 0), axis=-1)
    row = (start + rank // UNIT_ROWS) * UNIT_ROWS + rank % UNIT_ROWS
    slot = jnp.arange(TOP_K, dtype=I32)[:, None] * n_tok + jnp.arange(n_tok, dtype=I32)[None, :]
    n_rows = n_units * UNIT_ROWS
    row_slot = jnp.full((n_rows,), -1, I32).at[row.reshape(-1)].set(
        slot.reshape(-1), unique_indices=True, mode="promise_in_bounds")
    valid = row_slot >= 0
    row_tok = jnp.where(valid, row_slot % n_tok, 0).astype(I32)
    trash = TOP_K * n_tok + jnp.arange(n_rows, dtype=I32) % UNIT_ROWS
    row_dst = jnp.where(valid, row_slot, trash).astype(I32)
    return unit_e, unit_n, row_tok, row_dst


def _tile(n, want):
    want = min(n, want)
    assert n % want == 0, (n, want)
    return want


def _layer(h, p_i, ln_mix_g, w_in, pool_w, pool_scale, q_norm_g, k_norm_g, w_out, ln_ffn_g, router_w, router_b,
           w_gate_up, b_gate_up, w_down, b_down, ln_ple_g, ple_gate_w, ple_proj, ple_norm_g):
    b, s, d = h.shape
    t = b * s
    pool_width = d // 2
    attn_width = d // 2
    n_exp = router_w.shape[1]
    hidden = w_down.shape[1]
    assert pool_width % (LANES * len(POOL_WINDOWS)) == 0 and attn_width % LANES == 0
    x2 = h.reshape(t, d)

    proj = _in_proj(x2, ln_mix_g.reshape(1, d), w_in.astype(BF16), tm=_tile(t, 1024), tn=_tile(w_in.shape[1], 1024))
    proj3 = proj.reshape(b, s, -1)
    pool_out = _pool_mixer(proj3, pool_w.astype(BF16), pool_scale.reshape(1, pool_width), width=pool_width)
    attn_out = _stickbreak(proj3, q_norm_g, k_norm_g, pool_width=pool_width, attn_width=attn_width,
                           blk=_tile(s, 256))

    w_out_b = w_out.astype(BF16)
    h1, hn, top_e, gates, rank, counts = _mix_router(
        pool_out.reshape(t, pool_width), attn_out.reshape(t, attn_width), x2,
        w_out_b[:pool_width], w_out_b[pool_width:], ln_ffn_g.reshape(1, d),
        router_w.T.astype(BF16), router_b.reshape(n_exp, 1), tm=_tile(t, 512))

    n_units = t * TOP_K // UNIT_ROWS + n_exp + 2
    unit_e, unit_n, row_tok, row_dst = _routing_tables(top_e, rank, counts[:, 0], n_units=n_units, n_tok=t)
    wg, wu = _split_gate_up(w_gate_up, td=_tile(d, 256))
    bg = b_gate_up[:, 0::2].reshape(n_exp, 1, hidden)
    bu = b_gate_up[:, 1::2].reshape(n_exp, 1, hidden)
    slots = _moe_experts(hn, unit_e, unit_n, row_tok, row_dst, wg, wu, w_down, bg, bu,
                         b_down.reshape(n_exp, 1, d), n_slots=t * TOP_K + UNIT_ROWS, tf=_tile(hidden, 512))

    out = _ple_out(h1, slots, gates.T, p_i.reshape(t, -1),
                   ln_ple_g.reshape(1, d), ple_gate_w.astype(BF16), ple_proj.astype(BF16),
                   ple_norm_g.reshape(1, d), tm=_tile(t, 256))
    return out.reshape(b, s, d)


def kernel(x, p, ln_mix_g, w_in, pool_w, pool_scale, q_norm_g, k_norm_g, w_out, ln_ffn_g, router_w, router_b,
           w_gate_up, b_gate_up, w_down, b_down, ln_ple_g, ple_gate_w, ple_proj, ple_norm_g):
    h = x
    for i in range(p.shape[0]):
        h = _layer(h, p[i], ln_mix_g[i], w_in[i], pool_w[i], pool_scale[i], q_norm_g[i], k_norm_g[i], w_out[i],
                   ln_ffn_g[i], router_w[i], router_b[i], w_gate_up[i], b_gate_up[i], w_down[i], b_down[i],
                   ln_ple_g[i], ple_gate_w[i], ple_proj[i], ple_norm_g[i])
    return h
```
